```python
import jax
import jax.numpy as jnp
from jax import lax
import numpy as np


D_MODEL = 2048
BATCH = 16
SEQ = 2048
DEPTH = 2

GRID_W = 64
CTX_LEN = 256
HEAD_DIM = 128
Q_BLOCK = 128
ROPE_THETA = 10000.0
EPS = 1e-6

GQA_HEADS = 8
GQA_KV_HEADS = 2
GQA_GROUP = GQA_HEADS // GQA_KV_HEADS
GQA_Q_W = GQA_HEADS * HEAD_DIM
GQA_KV_W = GQA_KV_HEADS * HEAD_DIM
GQA_IN_W = GQA_Q_W + 2 * GQA_KV_W
HGRN_HEADS = 8
HGRN_DK = 128
HGRN_DV = 128
HGRN_KW = HGRN_HEADS * HGRN_DK
HGRN_VW = HGRN_HEADS * HGRN_DV
HGRN_IN_W = 3 * HGRN_KW + 2 * HGRN_VW
HGRN_CHUNK = 32
NA_HEADS = 8
NA_WIN_R_MAX = 8
NA_WIN_C = 16
NA_W = NA_HEADS * HEAD_DIM
NA_IN_W = 3 * NA_W
MLA_HEADS = 8
MLA_Q_RANK = 512
MLA_KV_RANK = 512
MLA_NOPE = 128
MLA_ROPE = 64
MLA_V = 128
MLA_IN_W = MLA_Q_RANK + MLA_KV_RANK + MLA_ROPE
PEER_HEADS = 8
PEER_N_KEYS = 128
PEER_N_EXPERTS = PEER_N_KEYS * PEER_N_KEYS
PEER_TOPK = 16
PEER_QDIM = 256
PEER_TOKEN_BLOCK = 128

N_EVEN = (DEPTH + 1) // 2
N_ODD = DEPTH // 2
EVEN_IN_W = GQA_IN_W + HGRN_IN_W
EVEN_OUT_W = GQA_Q_W + HGRN_VW
ODD_IN_W = NA_IN_W + MLA_IN_W
ODD_OUT_W = NA_W + MLA_HEADS * MLA_V

kernel_name = 'hybrid_flow_gqa_hgrn2_natten_mla_peer'


def rms_norm(x, w):
    xf = x.astype(jnp.float32)
    y = xf * lax.rsqrt(jnp.mean(xf * xf, axis=-1, keepdims=True) + EPS)
    return (y * w.astype(jnp.float32)).astype(x.dtype)


def modulate(h, shift, scale):
    return h * (1.0 + scale) + shift


def split_heads(z, h):
    b, l, _ = z.shape
    return z.reshape(b, l, h, -1).transpose(0, 2, 1, 3)


def merge_heads(z):
    b, h, l, d = z.shape
    return z.transpose(0, 2, 1, 3).reshape(b, l, h * d)


def rope_1d(x, pos):
    half = x.shape[-1] // 2
    inv = ROPE_THETA ** (-jnp.arange(half, dtype=jnp.float32) / half)
    ang = pos.astype(jnp.float32)[:, None] * inv[None, :]
    cos, sin = jnp.cos(ang), jnp.sin(ang)
    xf = x.astype(jnp.float32)
    x1, x2 = xf[..., :half], xf[..., half:]
    return jnp.concatenate([x1 * cos - x2 * sin, x1 * sin + x2 * cos], axis=-1).astype(x.dtype)


def axial_rope(x):
    n = x.shape[-2]
    t = jnp.arange(n)
    h = x.shape[-1] // 2
    return jnp.concatenate([rope_1d(x[..., :h], t // GRID_W), rope_1d(x[..., h:], t % GRID_W)], axis=-1)


def softmax_attend(q, k, v, scale):
    s = jnp.einsum('bhgqd,bhkd->bhgqk', q, k).astype(jnp.float32) * scale
    p = jax.nn.softmax(s, axis=-1).astype(v.dtype)
    return jnp.einsum('bhgqk,bhkd->bhgqd', p, v)


def blocked_attend(q, k, v, scale):
    b, hk, g, n, dk = q.shape
    qb = q.reshape(b, hk, g, n // Q_BLOCK, Q_BLOCK, dk).transpose(3, 0, 1, 2, 4, 5)
    o = lax.map(lambda qi: softmax_attend(qi, k, v, scale), qb)
    return o.transpose(1, 2, 3, 0, 4, 5).reshape(b, hk, g, n, v.shape[-1])


def gqa_mixer(z_x, z_c, q_norm_w, k_norm_w, need_ctx):
    def prep(z, rotate):
        q, k, v = jnp.split(z, [GQA_Q_W, GQA_Q_W + GQA_KV_W], axis=-1)
        q = rms_norm(split_heads(q, GQA_HEADS), q_norm_w)
        k = rms_norm(split_heads(k, GQA_KV_HEADS), k_norm_w)
        if rotate:
            q, k = axial_rope(q), axial_rope(k)
        b, _, l, d = q.shape
        return q.reshape(b, GQA_KV_HEADS, GQA_GROUP, l, d), k, split_heads(v, GQA_KV_HEADS)

    def out(o):
        b, hk, g, l, d = o.shape
        return merge_heads(o.reshape(b, hk * g, l, d))

    scale = HEAD_DIM ** -0.5
    qc, kc, vc = prep(z_c, False)
    qx, kx, vx = prep(z_x, True)
    y_x = out(blocked_attend(qx, jnp.concatenate([kc, kx], axis=2), jnp.concatenate([vc, vx], axis=2), scale))
    y_c = out(softmax_attend(qc, kc, vc, scale)) if need_ctx else None
    return y_x, y_c


def hgrn_forget(f_pre, lb):
    lb = lb.reshape(HGRN_HEADS, 1, HGRN_DK)
    s = jax.nn.sigmoid(f_pre.astype(jnp.float32))
    f = lb + (1.0 - lb) * s
    return (1.0 - lb) * (1.0 - s), jnp.log(f)


def hgrn_chunk_scan(q, k, g, v, s0):
    b, h, l, _ = q.shape
    nc = l // HGRN_CHUNK

    def to_chunks(z):
        return z.reshape(b, h, nc, HGRN_CHUNK, z.shape[-1]).transpose(2, 0, 1, 3, 4)

    incl = jnp.tril(jnp.ones((HGRN_CHUNK, HGRN_CHUNK), dtype=bool))[:, :, None]

    def step(s, inp):
        qc, kc, gc, vc = inp
        cum = jnp.cumsum(gc, axis=2)
        rel = cum[:, :, :, None, :] - cum[:, :, None, :, :]
        decay = jnp.exp(jnp.where(incl, rel, -jnp.inf))
        a = jnp.einsum('bhtd,bhsd,bhtsd->bhts', qc, kc, decay)
        o = jnp.einsum('bhts,bhsv->bhtv', a, vc) + jnp.einsum('bhtd,bhdv->bhtv', qc * jnp.exp(cum), s)
        last = cum[:, :, -1:, :]
        s_new = jnp.exp(last[:, :, 0, :])[..., None] * s + jnp.einsum('bhsd,bhsv->bhdv', kc * jnp.exp(last - cum), vc)
        return s_new, o

    s_fin, o = lax.scan(step, s0, (to_chunks(q), to_chunks(k), to_chunks(g), to_chunks(v)))
    return o.transpose(1, 2, 0, 3, 4).reshape(b, h, l, -1), s_fin


def hgrn_mixer(z_x, z_c, lb_fw, lb_bw, o_norm_w, need_ctx):
    splits = [HGRN_KW, 2 * HGRN_KW, 3 * HGRN_KW, 3 * HGRN_KW + HGRN_VW]

    def prep(z):
        q, f_fw, f_bw, i, g = jnp.split(z, splits, axis=-1)
        q = split_heads(q, HGRN_HEADS).astype(jnp.float32)
        i = split_heads(i, HGRN_HEADS).astype(jnp.float32)
        fw = hgrn_forget(split_heads(f_fw, HGRN_HEADS), lb_fw)
        bw = hgrn_forget(split_heads(f_bw, HGRN_HEADS), lb_bw)
        return q, i, fw, bw, g

    flip = lambda t: jnp.flip(t, axis=2)
    qc, ic, (kfc, gfc), (kbc, gbc), gate_c = prep(z_c)
    qx, ix, (kfx, gfx), (kbx, gbx), gate_x = prep(z_x)
    s0 = jnp.zeros((z_x.shape[0], HGRN_HEADS, HGRN_DK, HGRN_DV), jnp.float32)
    o_fc, s_fc = hgrn_chunk_scan(qc, kfc, gfc, ic, s0)
    o_bc, s_bc = hgrn_chunk_scan(flip(qc), flip(kbc), flip(gbc), flip(ic), s0)
    o_fx, _ = hgrn_chunk_scan(qx, kfx, gfx, ix, s_fc)
    o_bx, _ = hgrn_chunk_scan(flip(qx), flip(kbx), flip(gbx), flip(ix), s_bc)
    w = o_norm_w.reshape(HGRN_HEADS, 1, HGRN_DV)

    def readout(o, gate):
        return merge_heads(rms_norm(o, w)).astype(gate.dtype) * jax.nn.silu(gate)

    y_x = readout(o_fx + flip(o_bx), gate_x)
    y_c = readout(o_fc + flip(o_bc), gate_c) if need_ctx else None
    return y_x, y_c


def na_mixer(z_x, z_c, rpb, need_ctx):
    b, n, _ = z_x.shape
    rows = n // GRID_W
    win_r = min(NA_WIN_R_MAX, rows)
    scale = HEAD_DIM ** -0.5
    qx, kx, vx = [split_heads(t, NA_HEADS) for t in jnp.split(z_x, 3, axis=-1)]
    qc, kc, vc = [split_heads(t, NA_HEADS) for t in jnp.split(z_c, 3, axis=-1)]
    n_ctx = kc.shape[2]
    cols = jnp.arange(GRID_W)
    col_start = jnp.clip(cols - NA_WIN_C // 2, 0, GRID_W - NA_WIN_C)
    in_win = (cols[None, :] >= col_start[:, None]) & (cols[None, :] < col_start[:, None] + NA_WIN_C)
    win_mask = jnp.tile(in_win[:, None, :], (1, win_r, 1)).reshape(GRID_W, win_r * GRID_W)
    dc_idx = jnp.clip(cols[None, :] - cols[:, None] + NA_WIN_C - 1, 0, 2 * NA_WIN_C - 2)
    kg = kx.reshape(b, NA_HEADS, rows, GRID_W, HEAD_DIM)
    vg = vx.reshape(b, NA_HEADS, rows, GRID_W, HEAD_DIM)

    def row_block(args):
        r, q_row = args
        r0 = jnp.clip(r - win_r // 2, 0, rows - win_r)
        k_band = lax.dynamic_slice_in_dim(kg, r0, win_r, axis=2).reshape(b, NA_HEADS, win_r * GRID_W, HEAD_DIM)
        v_band = lax.dynamic_slice_in_dim(vg, r0, win_r, axis=2).reshape(b, NA_HEADS, win_r * GRID_W, HEAD_DIM)
        dr_idx = r0 + jnp.arange(win_r) - r + NA_WIN_R_MAX - 1
        bias = rpb[:, dr_idx][:, :, dc_idx]
        bias = bias.transpose(0, 2, 1, 3).reshape(NA_HEADS, GRID_W, win_r * GRID_W)
        s_loc = jnp.einsum('bhqd,bhkd->bhqk', q_row, k_band).astype(jnp.float32) * scale + bias
        s_loc = jnp.where(win_mask, s_loc, -jnp.inf)
        s_ctx = jnp.einsum('bhqd,bhkd->bhqk', q_row, kc).astype(jnp.float32) * scale
        p = jax.nn.softmax(jnp.concatenate([s_ctx, s_loc], axis=-1), axis=-1).astype(vx.dtype)
        return jnp.einsum('bhqk,bhkd->bhqd', p[..., :n_ctx], vc) + jnp.einsum('bhqk,bhkd->bhqd', p[..., n_ctx:], v_band)

    q_rows = qx.reshape(b, NA_HEADS, rows, GRID_W, HEAD_DIM).transpose(2, 0, 1, 3, 4)
    o = lax.map(row_block, (jnp.arange(rows), q_rows))
    y_x = merge_heads(o.transpose(1, 2, 0, 3, 4).reshape(b, NA_HEADS, n, HEAD_DIM))
    y_c = merge_heads(softmax_attend(qc[:, :, None], kc, vc, scale)[:, :, 0]) if need_ctx else None
    return y_x, y_c


def mla_mixer(z_x, z_c, q_norm_w, w_q_up, kv_norm_w, w_kv_up, need_ctx):
    def prep(z, rotate):
        cq, ckv, k_rope = jnp.split(z, [MLA_Q_RANK, MLA_Q_RANK + MLA_KV_RANK], axis=-1)
        q = split_heads(rms_norm(cq, q_norm_w) @ w_q_up, MLA_HEADS)
        kv = split_heads(rms_norm(ckv, kv_norm_w) @ w_kv_up, MLA_HEADS)
        q_nope, q_rope = q[..., :MLA_NOPE], q[..., MLA_NOPE:]
        k_nope, v = kv[..., :MLA_NOPE], kv[..., MLA_NOPE:]
        k_rope = k_rope[:, None]
        if rotate:
            q_rope, k_rope = axial_rope(q_rope), axial_rope(k_rope)
        k = jnp.concatenate([k_nope, jnp.broadcast_to(k_rope, k_nope.shape[:-1] + (MLA_ROPE,))], axis=-1)
        return jnp.concatenate([q_nope, q_rope], axis=-1)[:, :, None], k, v

    scale = (MLA_NOPE + MLA_ROPE) ** -0.5
    qc, kc, vc = prep(z_c, False)
    qx, kx, vx = prep(z_x, True)
    y_x = merge_heads(blocked_attend(qx, jnp.concatenate([kc, kx], axis=2), jnp.concatenate([vc, vx], axis=2), scale)[:, :, 0])
    y_c = merge_heads(softmax_attend(qc, kc, vc, scale)[:, :, 0]) if need_ctx else None
    return y_x, y_c


def peer_ffn(x, wq, keys, u, v):
    t, d = x.shape
    xb = x.reshape(t // PEER_TOKEN_BLOCK, PEER_TOKEN_BLOCK, d)

    def block(xt):
        q = (xt @ wq).reshape(-1, PEER_HEADS, 2, PEER_QDIM // 2)
        s = jnp.einsum('thpd,hpkd->thpk', q, keys).astype(jnp.float32)
        sv, si = lax.top_k(s, PEER_TOPK)
        cand = (sv[:, :, 0, :, None] + sv[:, :, 1, None, :]).reshape(-1, PEER_HEADS, PEER_TOPK * PEER_TOPK)
        cidx = (si[:, :, 0, :, None] * PEER_N_KEYS + si[:, :, 1, None, :]).reshape(-1, PEER_HEADS, PEER_TOPK * PEER_TOPK)
        top_s, pos = lax.top_k(cand, PEER_TOPK)
        eidx = jnp.take_along_axis(cidx, pos, axis=-1).reshape(-1, PEER_HEADS * PEER_TOPK)
        gate = jax.nn.softmax(top_s, axis=-1).reshape(-1, PEER_HEADS * PEER_TOPK).astype(xt.dtype)
        h = jax.nn.gelu(jnp.einsum('td,ted->te', xt, u[eidx]))
        return jnp.einsum('te,ted->td', gate * h, v[eidx])

    return lax.map(block, xb).reshape(t, d)


def setup_inputs(seed: int = 0) -> dict:
    key = jax.random.key(seed)
    ks = iter(jax.random.split(key, 32))
    D = D_MODEL

    def nrm(shape, s):
        return jax.random.normal(next(ks), shape, jnp.float32) * s

    def gain(shape):
        return 1.0 + nrm(shape, 0.02)

    return {
        'x': nrm((BATCH, SEQ, D), 1.0),
        'c': nrm((BATCH, D), 1.0),
        'ctx': nrm((BATCH, CTX_LEN, D), 1.0),
        'c_ctx': nrm((D,), 1.0),
        'w_mod': nrm((DEPTH, D, 6 * D), 0.5 * D ** -0.5),
        'b_mod': nrm((DEPTH, 6 * D), 0.01),
        'norm1_w': gain((DEPTH, D)),
        'norm2_w': gain((DEPTH, D)),
        'ev_w_in': nrm((N_EVEN, D, EVEN_IN_W), D ** -0.5),
        'ev_w_out': nrm((N_EVEN, EVEN_OUT_W, D), EVEN_OUT_W ** -0.5),
        'gqa_q_norm': gain((N_EVEN, HEAD_DIM)),
        'gqa_k_norm': gain((N_EVEN, HEAD_DIM)),
        'hgrn_lb': nrm((2, DEPTH + 1, HGRN_KW), 0.1),
        'hgrn_o_norm': gain((N_EVEN, HGRN_VW)),
        'od_w_in': nrm((N_ODD, D, ODD_IN_W), D ** -0.5),
        'od_w_out': nrm((N_ODD, ODD_OUT_W, D), ODD_OUT_W ** -0.5),
        'na_rpb': nrm((N_ODD, NA_HEADS, 2 * NA_WIN_R_MAX - 1, 2 * NA_WIN_C - 1), 0.1),
        'mla_q_norm': gain((N_ODD, MLA_Q_RANK)),
        'mla_w_q_up': nrm((N_ODD, MLA_Q_RANK, MLA_HEADS * (MLA_NOPE + MLA_ROPE)), MLA_Q_RANK ** -0.5),
        'mla_kv_norm': gain((N_ODD, MLA_KV_RANK)),
        'mla_w_kv_up': nrm((N_ODD, MLA_KV_RANK, MLA_HEADS * (MLA_NOPE + MLA_V)), MLA_KV_RANK ** -0.5),
        'peer_wq': nrm((DEPTH, D, PEER_HEADS * PEER_QDIM), D ** -0.5),
        'peer_keys': nrm((DEPTH, PEER_HEADS, 2, PEER_N_KEYS, PEER_QDIM // 2), (PEER_QDIM // 2) ** -0.5),
        'peer_u': nrm((DEPTH, PEER_N_EXPERTS, D), D ** -0.5),
        'peer_v': nrm((DEPTH, PEER_N_EXPERTS, D), 1.0),
        'final_norm_w': gain((D,)),
    }


def reference(x, c, ctx, c_ctx, w_mod, b_mod, norm1_w, norm2_w, ev_w_in, ev_w_out, gqa_q_norm, gqa_k_norm,
              hgrn_lb, hgrn_o_norm, od_w_in, od_w_out, na_rpb, mla_q_norm, mla_w_q_up, mla_kv_norm, mla_w_kv_up,
              peer_wq, peer_keys, peer_u, peer_v, final_norm_w):
    b, n, d = x.shape
    sc = jax.nn.silu(c)[:, None, :]
    sctx = jax.nn.silu(c_ctx)[None, None, :]
    lb_table = jnp.cumsum(jax.nn.softmax(hgrn_lb.astype(jnp.float32), axis=1), axis=1)
    h_x, h_c = x, ctx
    for l in range(DEPTH):
        need_ctx = l < DEPTH - 1
        j = l // 2
        mx = jnp.split(sc @ w_mod[l] + b_mod[l], 6, axis=-1)
        mc = jnp.split(sctx @ w_mod[l] + b_mod[l], 6, axis=-1)
        a_x = modulate(rms_norm(h_x, norm1_w[l]), mx[0], mx[1])
        a_c = modulate(rms_norm(h_c, norm1_w[l]), mc[0], mc[1])
        if l % 2 == 0:
            z_x, z_c = a_x @ ev_w_in[j], a_c @ ev_w_in[j]
            ya_x, ya_c = gqa_mixer(z_x[..., :GQA_IN_W], z_c[..., :GQA_IN_W], gqa_q_norm[j], gqa_k_norm[j], need_ctx)
            yb_x, yb_c = hgrn_mixer(z_x[..., GQA_IN_W:], z_c[..., GQA_IN_W:], lb_table[0, l], lb_table[1, l],
                                    hgrn_o_norm[j], need_ctx)
            w_out = ev_w_out[j]
        else:
            z_x, z_c = a_x @ od_w_in[j], a_c @ od_w_in[j]
            ya_x, ya_c = na_mixer(z_x[..., :NA_IN_W], z_c[..., :NA_IN_W], na_rpb[j], need_ctx)
            yb_x, yb_c = mla_mixer(z_x[..., NA_IN_W:], z_c[..., NA_IN_W:], mla_q_norm[j], mla_w_q_up[j],
                                   mla_kv_norm[j], mla_w_kv_up[j], need_ctx)
            w_out = od_w_out[j]
        h_x = h_x + mx[2] * (jnp.concatenate([ya_x, yb_x], axis=-1) @ w_out)
        f_x = modulate(rms_norm(h_x, norm2_w[l]), mx[3], mx[4])
        h_x = h_x + mx[5] * peer_ffn(f_x.reshape(-1, d), peer_wq[l], peer_keys[l], peer_u[l], peer_v[l]).reshape(b, n, d)
        if need_ctx:
            h_c = h_c + mc[2] * (jnp.concatenate([ya_c, yb_c], axis=-1) @ w_out)
            f_c = modulate(rms_norm(h_c, norm2_w[l]), mc[3], mc[4])
            h_c = h_c + mc[5] * peer_ffn(f_c.reshape(-1, d), peer_wq[l], peer_keys[l], peer_u[l], peer_v[l]).reshape(h_c.shape)
    return rms_norm(h_x, final_norm_w)
```

```python
import functools
import math

import jax
import jax.numpy as jnp
from jax import lax
from jax.experimental import pallas as pl
from jax.experimental.pallas import tpu as pltpu

F32 = jnp.float32
BF16 = jnp.bfloat16
I32 = jnp.int32

EPS = 1e-6
ROPE_THETA = 10000.0
GRID_W = 64
HEAD_DIM = 128
LANES = 128

GQA_HEADS = 8
GQA_KV_HEADS = 2
GQA_GROUP = GQA_HEADS // GQA_KV_HEADS
GQA_IN_W = (GQA_HEADS + 2 * GQA_KV_HEADS) * HEAD_DIM
HGRN_HEADS = 8
HGRN_CHUNK = 32
NA_HEADS = 8
NA_WIN_R_MAX = 8
NA_WIN_C = 16
NA_IN_W = 3 * NA_HEADS * HEAD_DIM
MLA_HEADS = 8
MLA_Q_RANK = 512
MLA_KV_RANK = 512
MLA_NOPE = 128
MLA_ROPE = 64
MLA_V = 128
MLA_QK_PAD = 256
PEER_HEADS = 8
PEER_N_KEYS = 128
PEER_TOPK = 16
PEER_HALF = 128

VMEM_LIMIT = 56 * 1024 * 1024
NEG_BIG = -1e30


def _cparams(sem):
    return pltpu.CompilerParams(dimension_semantics=sem, vmem_limit_bytes=VMEM_LIMIT)


def _mod_kernel(c_ref, w_ref, b_ref, o_ref):
    c = c_ref[...]
    a = (c * jax.nn.sigmoid(c)).astype(BF16)
    o_ref[...] = jnp.dot(a, w_ref[...].astype(BF16), preferred_element_type=F32) + b_ref[...]


def _modulation(c_all, w, b, tn=1024):
    m, d = c_all.shape
    n = w.shape[1]
    return pl.pallas_call(
        _mod_kernel,
        out_shape=jax.ShapeDtypeStruct((m, n), F32),
        grid=(n // tn,),
        in_specs=[pl.BlockSpec((m, d), lambda j: (0, 0)),
                  pl.BlockSpec((d, tn), lambda j: (0, j)),
                  pl.BlockSpec((1, tn), lambda j: (0, j))],
        out_specs=pl.BlockSpec((m, tn), lambda j: (0, j)),
        compiler_params=_cparams(("parallel",)),
        name="modulation",
    )(c_all, w, b.reshape(1, n))


def _nmm_kernel(*refs, modulate, shift_row, emit_a):
    if modulate:
        h_ref, nw_ref, mod_ref, w_ref = refs[:4]
        rest = refs[4:]
    else:
        h_ref, nw_ref, w_ref = refs[:3]
        mod_ref = None
        rest = refs[3:]
    if emit_a:
        o_ref, a_ref, a_s = rest
    else:
        o_ref, a_s = rest
        a_ref = None

    @pl.when(pl.program_id(2) == 0)
    def _():
        x = h_ref[0].astype(F32)
        y = x * lax.rsqrt(jnp.mean(x * x, axis=-1, keepdims=True) + EPS) * nw_ref[...]
        if modulate:
            shift = mod_ref[0, pl.ds(shift_row, 1), :]
            scale = mod_ref[0, pl.ds(shift_row + 1, 1), :]
            y = y * (1.0 + scale) + shift
        a_s[...] = y.astype(BF16)
        if emit_a:
            a_ref[0] = y.astype(BF16)

    o_ref[0] = jnp.dot(a_s[...], w_ref[...], preferred_element_type=F32).astype(o_ref.dtype)


def _norm_matmul(h, norm_w, w, *, mod=None, shift_row=0, n_ctx_blocks=0, col_block=0, row_block0=0,
                 n_rows=None, tm=256, tn=512, emit_a=False, out_dtype=F32, name="norm_matmul"):
    b, s, _ = h.shape
    k, n = w.shape
    n_rows = s if n_rows is None else n_rows
    nb = b
    modulate = mod is not None
    grid = (b, n_rows // tm, n // tn)
    in_specs = [pl.BlockSpec((1, tm, k), lambda bi, i, j: (bi, i + row_block0, col_block)),
                pl.BlockSpec((1, k), lambda bi, i, j: (0, 0))]
    args = [h, norm_w.reshape(1, k).astype(F32)]
    if modulate:
        in_specs.append(pl.BlockSpec((1, 6, k), lambda bi, i, j: (jnp.where(i < n_ctx_blocks, nb, bi), 0, 0)))
        args.append(mod)
    in_specs.append(pl.BlockSpec((k, tn), lambda bi, i, j: (0, j)))
    args.append(w)
    out_shape = [jax.ShapeDtypeStruct((b, n_rows, n), out_dtype)]
    out_specs = [pl.BlockSpec((1, tm, tn), lambda bi, i, j: (bi, i, j))]
    if emit_a:
        out_shape.append(jax.ShapeDtypeStruct((b, n_rows, k), BF16))
        out_specs.append(pl.BlockSpec((1, tm, k), lambda bi, i, j: (bi, i, 0)))
    res = pl.pallas_call(
        functools.partial(_nmm_kernel, modulate=modulate, shift_row=shift_row, emit_a=emit_a),
        out_shape=out_shape, grid=grid, in_specs=in_specs, out_specs=out_specs,
        scratch_shapes=[pltpu.VMEM((tm, k), BF16)],
        compiler_params=_cparams(("parallel", "parallel", "arbitrary")),
        name=name,
    )(*args)
    return res if emit_a else res[0]


def _outproj_kernel(ya_ref, yb_ref, wa_ref, wb_ref, h_ref, mod_ref, o_ref, *, gate_row):
    acc = jnp.dot(ya_ref[0], wa_ref[...], preferred_element_type=F32)
    acc = acc + jnp.dot(yb_ref[0], wb_ref[...], preferred_element_type=F32)
    gate = mod_ref[0, pl.ds(gate_row, 1), :]
    o_ref[0] = h_ref[0] + gate * acc


def _out_proj(ya, yb, w_out, h, mod, *, n_ctx_blocks, h_row_block0=0, tm=256, tn=512):
    b, n_rows, ka = ya.shape
    kb = yb.shape[2]
    d = w_out.shape[1]
    nb = b
    wa = w_out[:ka].astype(BF16)
    wb = w_out[ka:].astype(BF16)
    return pl.pallas_call(
        functools.partial(_outproj_kernel, gate_row=2),
        out_shape=jax.ShapeDtypeStruct((b, n_rows, d), F32),
        grid=(b, n_rows // tm, d // tn),
        in_specs=[pl.BlockSpec((1, tm, ka), lambda bi, i, j: (bi, i, 0)),
                  pl.BlockSpec((1, tm, kb), lambda bi, i, j: (bi, i, 0)),
                  pl.BlockSpec((ka, tn), lambda bi, i, j: (0, j)),
                  pl.BlockSpec((kb, tn), lambda bi, i, j: (0, j)),
                  pl.BlockSpec((1, tm, tn), lambda bi, i, j: (bi, i + h_row_block0, j)),
                  pl.BlockSpec((1, 6, tn), lambda bi, i, j: (jnp.where(i < n_ctx_blocks, nb, bi), 0, j))],
        out_specs=pl.BlockSpec((1, tm, tn), lambda bi, i, j: (bi, i, j)),
        compiler_params=_cparams(("parallel", "parallel", "parallel")),
        name="out_proj",
    )(ya, yb, wa, wb, h, mod)


def _rope_tables(n_ctx, n_x, width, half):
    t = jnp.arange(n_x)
    lane = jnp.arange(LANES)
    inv = ROPE_THETA ** (-(lane % half).astype(F32) / half)
    pos = jnp.where(lane[None, :] < width // 2, (t // GRID_W)[:, None], (t % GRID_W)[:, None]).astype(F32)
    ang = pos * inv[None, :]
    valid = (lane < width)[None, :]
    cos = jnp.where(valid, jnp.cos(ang), 1.0)
    first = (lane % (2 * half)) < half
    sin = jnp.where(valid, jnp.where(first[None, :], -jnp.sin(ang), jnp.sin(ang)), 0.0)
    cos = jnp.concatenate([jnp.ones((n_ctx, LANES), F32), cos], axis=0)
    sin = jnp.concatenate([jnp.zeros((n_ctx, LANES), F32), sin], axis=0)
    return cos, sin


def _rotate(x, cos, sin, half):
    lane = lax.broadcasted_iota(I32, x.shape, 1)
    first = (lane % (2 * half)) < half
    partner = jnp.where(first, pltpu.roll(x, LANES - half, 1), pltpu.roll(x, half, 1))
    return x * cos + partner * sin


def _gqa_prep_kernel(z_ref, qw_ref, kw_ref, cos_ref, sin_ref, o_ref):
    c = pl.program_id(2)
    x = z_ref[0]

    def normed(w):
        y = x * lax.rsqrt(jnp.mean(x * x, axis=-1, keepdims=True) + EPS) * w
        return _rotate(y, cos_ref[...], sin_ref[...], HEAD_DIM // 4)

    @pl.when(c < GQA_HEADS)
    def _():
        o_ref[0] = normed(qw_ref[...]).astype(BF16)

    @pl.when(jnp.logical_and(c >= GQA_HEADS, c < GQA_HEADS + GQA_KV_HEADS))
    def _():
        o_ref[0] = normed(kw_ref[...]).astype(BF16)

    @pl.when(c >= GQA_HEADS + GQA_KV_HEADS)
    def _():
        o_ref[0] = x.astype(BF16)


def _gqa_prep(z, q_norm_w, k_norm_w, cos, sin, ts=256):
    b, s, _ = z.shape
    ncol = GQA_IN_W // HEAD_DIM
    return pl.pallas_call(
        _gqa_prep_kernel,
        out_shape=jax.ShapeDtypeStruct((b, s, GQA_IN_W), BF16),
        grid=(b, s // ts, ncol),
        in_specs=[pl.BlockSpec((1, ts, HEAD_DIM), lambda bi, i, c: (bi, i, c)),
                  pl.BlockSpec((1, HEAD_DIM), lambda bi, i, c: (0, 0)),
                  pl.BlockSpec((1, HEAD_DIM), lambda bi, i, c: (0, 0)),
                  pl.BlockSpec((ts, LANES), lambda bi, i, c: (i, 0)),
                  pl.BlockSpec((ts, LANES), lambda bi, i, c: (i, 0))],
        out_specs=pl.BlockSpec((1, ts, HEAD_DIM), lambda bi, i, c: (bi, i, c)),
        compiler_params=_cparams(("parallel", "parallel", "parallel")),
        name="gqa_prep",
    )(z, q_norm_w.reshape(1, HEAD_DIM), k_norm_w.reshape(1, HEAD_DIM), cos, sin)


def _attn_kernel(q_ref, k_ref, v_ref, o_ref, *, scale, n_ctx_q_blocks, ctx_len, q_axis):
    q = q_ref[0]

    def attend(n_keys):
        k = k_ref[0, pl.ds(0, n_keys), :]
        v = v_ref[0, pl.ds(0, n_keys), :]
        s = lax.dot_general(q, k, (((1,), (1,)), ((), ())), preferred_element_type=F32) * scale
        m = jnp.max(s, axis=-1, keepdims=True)
        p = jnp.exp(s - m)
        l = jnp.sum(p, axis=-1, keepdims=True)
        o = jnp.dot(p.astype(BF16), v, preferred_element_type=F32)
        o_ref[0] = (o / l).astype(o_ref.dtype)

    n_all = k_ref.shape[1]
    if n_ctx_q_blocks == 0:
        attend(n_all)
    else:
        qi = pl.program_id(q_axis)

        @pl.when(qi < n_ctx_q_blocks)
        def _():
            attend(ctx_len)

        @pl.when(qi >= n_ctx_q_blocks)
        def _():
            attend(n_all)


def _attention(q_arr, k_arr, v_arr, *, n_heads, q_col, k_col, v_col, dk, dv, scale, q_row_block0,
               n_q_rows, n_ctx_q_blocks, ctx_len, tq=256, name="attention"):
    b, s, _ = k_arr.shape
    return pl.pallas_call(
        functools.partial(_attn_kernel, scale=scale, n_ctx_q_blocks=n_ctx_q_blocks, ctx_len=ctx_len, q_axis=2),
        out_shape=jax.ShapeDtypeStruct((b, n_q_rows, n_heads * dv), BF16),
        grid=(b, n_heads, n_q_rows // tq),
        in_specs=[pl.BlockSpec((1, tq, dk), lambda bi, h, i: (bi, i + q_row_block0, q_col(h))),
                  pl.BlockSpec((1, s, dk), lambda bi, h, i: (bi, 0, k_col(h))),
                  pl.BlockSpec((1, s, dv), lambda bi, h, i: (bi, 0, v_col(h)))],
        out_specs=pl.BlockSpec((1, tq, dv), lambda bi, h, i: (bi, i, h)),
        compiler_params=_cparams(("parallel", "parallel", "parallel")),
        name=name,
    )(q_arr, k_arr, v_arr)


def _split3(x):
    hi = x.astype(BF16)
    r1 = x - hi.astype(F32)
    mid = r1.astype(BF16)
    lo = (r1 - mid.astype(F32)).astype(BF16)
    return hi, mid, lo


def _exact_dot(a_bf16, x):
    hi, mid, lo = _split3(x)
    return (jnp.dot(a_bf16, hi, preferred_element_type=F32)
            + jnp.dot(a_bf16, mid, preferred_element_type=F32)
            + jnp.dot(a_bf16, lo, preferred_element_type=F32))


def _hgrn_kernel(q_ref, ff_ref, fb_ref, i_ref, g_ref, lbf_ref, lbb_ref, ow_ref, o_ref,
                 of_s, cum_s, kk_s, st_s, *, n_ctx_chunks, n_chunks):
    c_len = HGRN_CHUNK
    row = lax.broadcasted_iota(I32, (c_len, c_len), 0)
    col = lax.broadcasted_iota(I32, (c_len, c_len), 1)
    tri_f = (col <= row).astype(BF16)
    tri_b = (col >= row).astype(BF16)
    t_idx = lax.broadcasted_iota(I32, (c_len, HEAD_DIM), 0)

    def chunk(r0, f_ref, lb_ref, forward):
        rows = pl.ds(r0, c_len)
        q = q_ref[0, rows, :]
        v = i_ref[0, rows, :]
        lb = lb_ref[0]
        sg = jax.nn.sigmoid(f_ref[0, rows, :])
        k = (1.0 - lb) * (1.0 - sg)
        g = jnp.log(lb + (1.0 - lb) * sg)
        cum = _exact_dot(tri_f if forward else tri_b, g)
        total = jnp.sum(g, axis=0, keepdims=True)
        cum_s[...] = cum
        kk_s[...] = k
        o = jnp.zeros((c_len, HEAD_DIM), F32)
        for s in range(c_len):
            cs = cum_s[pl.ds(s, 1), :]
            ks = kk_s[pl.ds(s, 1), :]
            keep = (t_idx >= s) if forward else (t_idx <= s)
            w = q * ks * jnp.exp(jnp.where(keep, cum - cs, -jnp.inf))
            a = jnp.sum(w, axis=-1, keepdims=True)
            o = o + a * v[s:s + 1, :]
        st = st_s[...]
        qd = (q * jnp.exp(cum)).astype(BF16)
        o = o + lax.dot_general(qd, st.astype(BF16), (((1,), (1,)), ((), ())), preferred_element_type=F32)
        kd = (k * jnp.exp(total - cum)).astype(BF16)
        st_s[...] = st * jnp.exp(total) + lax.dot_general(v.astype(BF16), kd, (((0,), (0,)), ((), ())),
                                                         preferred_element_type=F32)
        return o

    st_s[...] = jnp.zeros_like(st_s)

    def fwd_body(c, carry):
        r0 = pl.multiple_of(c * c_len, c_len)
        of_s[pl.ds(r0, c_len), :] = chunk(r0, ff_ref, lbf_ref, True)
        return carry

    lax.fori_loop(0, n_chunks, fwd_body, 0)

    st_s[...] = jnp.zeros_like(st_s)

    def bwd_body(j, carry):
        c = jnp.where(j < n_ctx_chunks, n_ctx_chunks - 1 - j, n_chunks - 1 - (j - n_ctx_chunks))
        r0 = pl.multiple_of(c * c_len, c_len)
        rows = pl.ds(r0, c_len)
        o = of_s[rows, :] + chunk(r0, fb_ref, lbb_ref, False)
        y = o * lax.rsqrt(jnp.mean(o * o, axis=-1, keepdims=True) + EPS) * ow_ref[0]
        gate = g_ref[0, rows, :]
        o_ref[0, rows, :] = (y * (gate * jax.nn.sigmoid(gate))).astype(o_ref.dtype)
        return carry

    lax.fori_loop(0, n_chunks, bwd_body, 0)


def _hgrn(z, col0, lb_fw, lb_bw, o_norm_w, ctx_len):
    b, s, _ = z.shape
    nh = HGRN_HEADS
    blk = lambda grp: pl.BlockSpec((1, s, HEAD_DIM), lambda bi, h: (bi, 0, col0 + grp * nh + h))
    vec = pl.BlockSpec((1, 1, HEAD_DIM), lambda bi, h: (h, 0, 0))
    return pl.pallas_call(
        functools.partial(_hgrn_kernel, n_ctx_chunks=ctx_len // HGRN_CHUNK, n_chunks=s // HGRN_CHUNK),
        out_shape=jax.ShapeDtypeStruct((b, s, nh * HEAD_DIM), BF16),
        grid=(b, nh),
        in_specs=[blk(0), blk(1), blk(2), blk(3), blk(4), vec, vec, vec],
        out_specs=pl.BlockSpec((1, s, HEAD_DIM), lambda bi, h: (bi, 0, h)),
        scratch_shapes=[pltpu.VMEM((s, HEAD_DIM), F32),
                        pltpu.VMEM((HGRN_CHUNK, HEAD_DIM), F32),
                        pltpu.VMEM((HGRN_CHUNK, HEAD_DIM), F32),
                        pltpu.VMEM((HEAD_DIM, HEAD_DIM), F32)],
        compiler_params=_cparams(("parallel", "parallel")),
        name="hgrn",
    )(z, z, z, z, z, lb_fw.reshape(nh, 1, HEAD_DIM), lb_bw.reshape(nh, 1, HEAD_DIM),
      o_norm_w.reshape(nh, 1, HEAD_DIM))


def _na_kernel(q_ref, k_ref, v_ref, bias_ref, o_ref, *, scale, ctx_len, rows, win_r):
    r = pl.program_id(2)
    r0 = jnp.clip(r - win_r // 2, 0, rows - win_r)
    start = pl.multiple_of(ctx_len + r0 * GRID_W, GRID_W)
    band = pl.ds(start, win_r * GRID_W)
    q = q_ref[0].astype(BF16)
    kc = k_ref[0, pl.ds(0, ctx_len), :].astype(BF16)
    vc = v_ref[0, pl.ds(0, ctx_len), :].astype(BF16)
    kb = k_ref[0, band, :].astype(BF16)
    vb = v_ref[0, band, :].astype(BF16)
    nt = (((1,), (1,)), ((), ()))
    s_ctx = lax.dot_general(q, kc, nt, preferred_element_type=F32) * scale
    s_loc = lax.dot_general(q, kb, nt, preferred_element_type=F32) * scale + bias_ref[0, 0]
    m = jnp.maximum(jnp.max(s_ctx, axis=-1, keepdims=True), jnp.max(s_loc, axis=-1, keepdims=True))
    p_ctx = jnp.exp(s_ctx - m)
    p_loc = jnp.exp(s_loc - m)
    l = jnp.sum(p_ctx, axis=-1, keepdims=True) + jnp.sum(p_loc, axis=-1, keepdims=True)
    o = jnp.dot(p_ctx.astype(BF16), vc, preferred_element_type=F32)
    o = o + jnp.dot(p_loc.astype(BF16), vb, preferred_element_type=F32)
    o_ref[0] = (o / l).astype(o_ref.dtype)


def _na_bias_table(rpb, rows, win_r):
    cols = jnp.arange(GRID_W)
    col_start = jnp.clip(cols - NA_WIN_C // 2, 0, GRID_W - NA_WIN_C)
    in_win = (cols[None, :] >= col_start[:, None]) & (cols[None, :] < col_start[:, None] + NA_WIN_C)
    dc_idx = jnp.clip(cols[None, :] - cols[:, None] + NA_WIN_C - 1, 0, 2 * NA_WIN_C - 2)
    n_off = win_r
    off = jnp.arange(n_off)
    dr_idx = jnp.arange(win_r)[None, :] - off[:, None] + NA_WIN_R_MAX - 1
    dr_ok = (dr_idx >= 0) & (dr_idx <= 2 * NA_WIN_R_MAX - 2)
    bias = rpb[:, jnp.clip(dr_idx, 0, 2 * NA_WIN_R_MAX - 2)][:, :, :, dc_idx]
    keep = in_win[None, None, None] & dr_ok[None, :, :, None, None]
    bias = jnp.where(keep, bias, NEG_BIG)
    bias = bias.transpose(0, 1, 3, 2, 4).reshape(rpb.shape[0], n_off, GRID_W, win_r * GRID_W)
    return bias.astype(F32)


def _na(z, rpb, ctx_len, n_x):
    b, s, _ = z.shape
    rows = n_x // GRID_W
    win_r = min(NA_WIN_R_MAX, rows)
    bias = _na_bias_table(rpb, rows, win_r)
    nh = NA_HEADS
    ctx_blocks = ctx_len // GRID_W

    def off(r):
        return r - jnp.clip(r - win_r // 2, 0, rows - win_r)

    return pl.pallas_call(
        functools.partial(_na_kernel, scale=HEAD_DIM ** -0.5, ctx_len=ctx_len, rows=rows, win_r=win_r),
        out_shape=jax.ShapeDtypeStruct((b, n_x, nh * HEAD_DIM), BF16),
        grid=(b, nh, rows),
        in_specs=[pl.BlockSpec((1, GRID_W, HEAD_DIM), lambda bi, h, r: (bi, r + ctx_blocks, h)),
                  pl.BlockSpec((1, s, HEAD_DIM), lambda bi, h, r: (bi, 0, nh + h)),
                  pl.BlockSpec((1, s, HEAD_DIM), lambda bi, h, r: (bi, 0, 2 * nh + h)),
                  pl.BlockSpec((1, 1, GRID_W, win_r * GRID_W), lambda bi, h, r: (h, off(r), 0, 0))],
        out_specs=pl.BlockSpec((1, GRID_W, HEAD_DIM), lambda bi, h, r: (bi, r, h)),
        compiler_params=_cparams(("parallel", "parallel", "parallel")),
        name="na",
    )(z, z, z, bias)


def _mla_prep_kernel(qup_ref, kvup_ref, kr_ref, cos_ref, sin_ref, qf_ref, kf_ref, vf_ref):
    cos = cos_ref[...]
    sin = sin_ref[...]
    qf_ref[0, :, :LANES] = qup_ref[0, :, :LANES].astype(BF16)
    qf_ref[0, :, LANES:] = _rotate(qup_ref[0, :, LANES:], cos, sin, MLA_ROPE // 4).astype(BF16)
    kf_ref[0, :, :LANES] = kvup_ref[0, :, :LANES].astype(BF16)
    kf_ref[0, :, LANES:] = _rotate(kr_ref[0], cos, sin, MLA_ROPE // 4).astype(BF16)
    vf_ref[0] = kvup_ref[0, :, LANES:].astype(BF16)


def _mla_prep(qup, kvup, z, kr_col, cos, sin, ts=256):
    b, s, _ = qup.shape
    nh = MLA_HEADS
    return pl.pallas_call(
        _mla_prep_kernel,
        out_shape=[jax.ShapeDtypeStruct((b, s, nh * MLA_QK_PAD), BF16),
                   jax.ShapeDtypeStruct((b, s, nh * MLA_QK_PAD), BF16),
                   jax.ShapeDtypeStruct((b, s, nh * MLA_V), BF16)],
        grid=(b, s // ts, nh),
        in_specs=[pl.BlockSpec((1, ts, MLA_QK_PAD), lambda bi, i, h: (bi, i, h)),
                  pl.BlockSpec((1, ts, MLA_NOPE + MLA_V), lambda bi, i, h: (bi, i, h)),
                  pl.BlockSpec((1, ts, LANES), lambda bi, i, h: (bi, i, kr_col)),
                  pl.BlockSpec((ts, LANES), lambda bi, i, h: (i, 0)),
                  pl.BlockSpec((ts, LANES), lambda bi, i, h: (i, 0))],
        out_specs=[pl.BlockSpec((1, ts, MLA_QK_PAD), lambda bi, i, h: (bi, i, h)),
                   pl.BlockSpec((1, ts, MLA_QK_PAD), lambda bi, i, h: (bi, i, h)),
                   pl.BlockSpec((1, ts, MLA_V), lambda bi, i, h: (bi, i, h))],
        compiler_params=_cparams(("parallel", "parallel", "parallel")),
        name="mla_prep",
    )(qup, kvup, z, cos, sin)


def _topk_lanes(x, k, payload=None):
    rows, width = x.shape
    lane = lax.broadcasted_iota(I32, (rows, width), 1).astype(F32)
    out_lane = lax.broadcasted_iota(I32, (rows, k), 1)
    vals = jnp.zeros((rows, k), F32)
    poss = jnp.zeros((rows, k), F32)
    pays = jnp.zeros((rows, k), F32)
    for r in range(k):
        m = jnp.max(x, axis=-1, keepdims=True)
        pos = jnp.min(jnp.where(x == m, lane, float(width)), axis=-1, keepdims=True)
        hit = lane == pos
        vals = jnp.where(out_lane == r, m, vals)
        poss = jnp.where(out_lane == r, pos, poss)
        if payload is not None:
            pay = jnp.max(jnp.where(hit, payload, -1.0), axis=-1, keepdims=True)
            pays = jnp.where(out_lane == r, pay, pays)
        x = jnp.where(hit, -jnp.inf, x)
    return vals, poss, pays


def _peer_route_kernel(q_ref, keys_ref, gate_ref, eidx_ref):
    kk = PEER_TOPK
    rep_r = lax.broadcasted_iota(I32, (kk, kk * kk), 0)
    rep_c = lax.broadcasted_iota(I32, (kk, kk * kk), 1)
    e_outer = (rep_c // kk == rep_r).astype(BF16)
    e_inner = (rep_c % kk == rep_r).astype(BF16)
    nt = (((1,), (1,)), ((), ()))
    for h in range(PEER_HEADS):
        sv, si = [], []
        for p in range(2):
            j = 2 * h + p
            qj = q_ref[:, j * PEER_HALF:(j + 1) * PEER_HALF].astype(BF16)
            s = lax.dot_general(qj, keys_ref[j], nt, preferred_element_type=F32)
            v, pos, _ = _topk_lanes(s, kk)
            sv.append(v)
            si.append(pos)
        cand = _exact_dot_rhs(sv[0], e_outer) + _exact_dot_rhs(sv[1], e_inner)
        cidx = (jnp.dot(si[0].astype(BF16), e_outer, preferred_element_type=F32) * float(PEER_N_KEYS)
                + jnp.dot(si[1].astype(BF16), e_inner, preferred_element_type=F32))
        top_s, _, eidx = _topk_lanes(cand, kk, payload=cidx)
        e = jnp.exp(top_s - jnp.max(top_s, axis=-1, keepdims=True))
        gate_ref[h] = e / jnp.sum(e, axis=-1, keepdims=True)
        eidx_ref[h] = eidx.astype(I32)


def _exact_dot_rhs(x, e_bf16):
    hi, mid, lo = _split3(x)
    return (jnp.dot(hi, e_bf16, preferred_element_type=F32)
            + jnp.dot(mid, e_bf16, preferred_element_type=F32)
            + jnp.dot(lo, e_bf16, preferred_element_type=F32))


def _peer_route(q, keys, tm=256):
    t = q.shape[0]
    nh, kk = PEER_HEADS, PEER_TOPK
    gate, eidx = pl.pallas_call(
        _peer_route_kernel,
        out_shape=[jax.ShapeDtypeStruct((nh, t, kk), F32), jax.ShapeDtypeStruct((nh, t, kk), I32)],
        grid=(t // tm,),
        in_specs=[pl.BlockSpec((tm, q.shape[1]), lambda i: (i, 0)),
                  pl.BlockSpec(keys.shape, lambda i: (0, 0, 0))],
        out_specs=[pl.BlockSpec((nh, tm, kk), lambda i: (0, i, 0)),
                   pl.BlockSpec((nh, tm, kk), lambda i: (0, i, 0))],
        compiler_params=_cparams(("parallel",)),
        name="peer_route",
    )(q, keys)
    gate = gate.transpose(1, 0, 2).reshape(t, nh * kk)
    eidx = eidx.transpose(1, 0, 2).reshape(t, nh * kk)
    return gate, eidx


def _gelu_tanh(x):
    return 0.5 * x * (1.0 + jnp.tanh(math.sqrt(2.0 / math.pi) * (x + 0.044715 * x * x * x)))


def _peer_expert_kernel(idx_cur, idx_nxt, f_ref, gate_ref, h_ref, mod_ref, fw_ref, u_hbm, v_hbm, o_ref,
                        ubuf, vbuf, sem, *, tb, n_sel, final_norm):
    i = pl.program_id(0)
    n = pl.num_programs(0)
    slot = i % 2

    def row_copies(idx_ref, s, k):
        t = k // n_sel
        e = k % n_sel
        row = idx_ref[t, e]
        cu = pltpu.make_async_copy(u_hbm.at[pl.ds(row, 1), :], ubuf.at[s, t, pl.ds(e, 1), :], sem.at[0, s])
        cv = pltpu.make_async_copy(v_hbm.at[pl.ds(row, 1), :], vbuf.at[s, t, pl.ds(e, 1), :], sem.at[1, s])
        return cu, cv

    def issue(idx_ref, s):
        def body(k, carry):
            cu, cv = row_copies(idx_ref, s, k)
            cu.start()
            cv.start()
            return carry
        lax.fori_loop(0, tb * n_sel, body, 0, unroll=8)

    @pl.when(i == 0)
    def _():
        issue(idx_cur, 0)

    @pl.when(i + 1 < n)
    def _():
        issue(idx_nxt, 1 - slot)

    def wait_body(k, carry):
        cu, cv = row_copies(idx_cur, slot, k)
        cu.wait()
        cv.wait()
        return carry
    lax.fori_loop(0, tb * n_sel, wait_body, 0, unroll=8)

    gate_mod = mod_ref[0, pl.ds(5, 1), :]
    nt = (((1,), (1,)), ((), ()))
    for t in range(tb):
        x = f_ref[pl.ds(t, 1), :]
        u_t = ubuf[slot, t].astype(BF16)
        hh = lax.dot_general(x, u_t, nt, preferred_element_type=F32)
        w = gate_ref[pl.ds(t, 1), :] * _gelu_tanh(hh)
        y = jnp.dot(w.astype(BF16), vbuf[slot, t].astype(BF16), preferred_element_type=F32)
        out = h_ref[pl.ds(t, 1), :] + gate_mod * y
        if final_norm:
            out = out * lax.rsqrt(jnp.mean(out * out, axis=-1, keepdims=True) + EPS) * fw_ref[...]
        o_ref[pl.ds(t, 1), :] = out


def _peer_experts(f, gate, eidx, h, mod, u, v, final_w, *, rows_per_batch, n_ctx_rows, final_norm, tb=8):
    t, d = f.shape
    n_sel = gate.shape[1]
    nb = mod.shape[0] - 1
    bpb = rows_per_batch // tb
    ctx_blocks = n_ctx_rows // tb
    n_blocks = t // tb

    def mod_map(i):
        return (jnp.where(i % bpb < ctx_blocks, nb, i // bpb), 0, 0)

    smem = pltpu.SMEM
    return pl.pallas_call(
        functools.partial(_peer_expert_kernel, tb=tb, n_sel=n_sel, final_norm=final_norm),
        out_shape=jax.ShapeDtypeStruct((t, d), F32),
        grid=(n_blocks,),
        in_specs=[pl.BlockSpec((tb, n_sel), lambda i: (i, 0), memory_space=smem),
                  pl.BlockSpec((tb, n_sel), lambda i: (jnp.minimum(i + 1, n_blocks - 1), 0), memory_space=smem),
                  pl.BlockSpec((tb, d), lambda i: (i, 0)),
                  pl.BlockSpec((tb, n_sel), lambda i: (i, 0)),
                  pl.BlockSpec((tb, d), lambda i: (i, 0)),
                  pl.BlockSpec((1, 6, d), mod_map),
                  pl.BlockSpec((1, d), lambda i: (0, 0)),
                  pl.BlockSpec(memory_space=pl.ANY),
                  pl.BlockSpec(memory_space=pl.ANY)],
        out_specs=pl.BlockSpec((tb, d), lambda i: (i, 0)),
        scratch_shapes=[pltpu.VMEM((2, tb, n_sel, d), u.dtype),
                        pltpu.VMEM((2, tb, n_sel, d), v.dtype),
                        pltpu.SemaphoreType.DMA((2, 2))],
        compiler_params=_cparams(("arbitrary",)),
        name="peer_experts",
    )(eidx, eidx, f, gate, h, mod, final_w.reshape(1, d).astype(F32), u, v)


def _peer(h, mod, norm_w, wq, keys, u, v, final_w, *, n_ctx_blocks, n_ctx_rows, final_norm, tm=256):
    b, r, d = h.shape
    q, f = _norm_matmul(h, norm_w, wq.astype(BF16), mod=mod, shift_row=3, n_ctx_blocks=n_ctx_blocks,
                        tm=tm, emit_a=True, name="peer_query")
    keys2 = keys.reshape(PEER_HEADS * 2, PEER_N_KEYS, PEER_HALF).astype(BF16)
    gate, eidx = _peer_route(q.reshape(b * r, -1), keys2)
    out = _peer_experts(f.reshape(b * r, d), gate, eidx, h.reshape(b * r, d), mod, u, v, final_w,
                        rows_per_batch=r, n_ctx_rows=n_ctx_rows, final_norm=final_norm)
    return out.reshape(b, r, d)


def kernel(x, c, ctx, c_ctx, w_mod, b_mod, norm1_w, norm2_w, ev_w_in, ev_w_out, gqa_q_norm, gqa_k_norm, hgrn_lb, hgrn_o_norm, od_w_in, od_w_out, na_rpb, mla_q_norm, mla_w_q_up, mla_kv_norm, mla_w_kv_up, peer_wq, peer_keys, peer_u, peer_v, final_norm_w):
    b, n_x, d = x.shape
    n_ctx = ctx.shape[1]
    depth = w_mod.shape[0]
    tm = 256
    assert n_ctx % tm == 0 and n_x % tm == 0 and depth == 2
    ctx_blocks = n_ctx // tm

    h = jnp.concatenate([ctx, x], axis=1)
    c_all = jnp.concatenate([c, c_ctx[None, :]], axis=0)
    m_pad = -(-c_all.shape[0] // 8) * 8
    c_pad = jnp.pad(c_all, ((0, m_pad - c_all.shape[0]), (0, 0)))
    lb_table = jnp.cumsum(jax.nn.softmax(hgrn_lb.astype(F32), axis=1), axis=1)

    mod0 = _modulation(c_pad, w_mod[0], b_mod[0])[:b + 1].reshape(b + 1, 6, d)
    z = _norm_matmul(h, norm1_w[0], ev_w_in[0].astype(BF16), mod=mod0, shift_row=0, n_ctx_blocks=ctx_blocks,
                     tm=tm, name="in_proj0")
    cos_g, sin_g = _rope_tables(n_ctx, n_x, HEAD_DIM, HEAD_DIM // 4)
    qkv = _gqa_prep(z, gqa_q_norm[0], gqa_k_norm[0], cos_g, sin_g)
    ya = _attention(qkv, qkv, qkv, n_heads=GQA_HEADS, q_col=lambda hh: hh,
                    k_col=lambda hh: GQA_HEADS + hh // GQA_GROUP,
                    v_col=lambda hh: GQA_HEADS + GQA_KV_HEADS + hh // GQA_GROUP,
                    dk=HEAD_DIM, dv=HEAD_DIM, scale=HEAD_DIM ** -0.5, q_row_block0=0, n_q_rows=n_ctx + n_x,
                    n_ctx_q_blocks=ctx_blocks, ctx_len=n_ctx, tq=tm, name="gqa_attention")
    yb = _hgrn(z, GQA_IN_W // HEAD_DIM, lb_table[0, 0], lb_table[1, 0], hgrn_o_norm[0], n_ctx)
    h = _out_proj(ya, yb, ev_w_out[0], h, mod0, n_ctx_blocks=ctx_blocks, tm=tm)
    h = _peer(h, mod0, norm2_w[0], peer_wq[0], peer_keys[0], peer_u[0], peer_v[0], final_norm_w,
              n_ctx_blocks=ctx_blocks, n_ctx_rows=n_ctx, final_norm=False, tm=tm)

    mod1 = _modulation(c_pad, w_mod[1], b_mod[1])[:b + 1].reshape(b + 1, 6, d)
    odd_w = od_w_in[0]
    kr_pad = LANES - MLA_ROPE
    odd_w = jnp.pad(odd_w, ((0, 0), (0, kr_pad))).astype(BF16)
    z1 = _norm_matmul(h, norm1_w[1], odd_w, mod=mod1, shift_row=0, n_ctx_blocks=ctx_blocks, tm=tm,
                      tn=odd_w.shape[1] // 3, name="in_proj1")
    ya1 = _na(z1, na_rpb[0], n_ctx, n_x)

    qw = mla_w_q_up[0].reshape(MLA_Q_RANK, MLA_HEADS, MLA_NOPE + MLA_ROPE)
    qw = jnp.pad(qw, ((0, 0), (0, 0), (0, MLA_QK_PAD - MLA_NOPE - MLA_ROPE)))
    qw = qw.reshape(MLA_Q_RANK, MLA_HEADS * MLA_QK_PAD).astype(BF16)
    cq_col = NA_IN_W // MLA_Q_RANK
    qup = _norm_matmul(z1, mla_q_norm[0], qw, col_block=cq_col, tm=tm, name="mla_q_up")
    kvup = _norm_matmul(z1, mla_kv_norm[0], mla_w_kv_up[0].astype(BF16), col_block=cq_col + 1, tm=tm,
                        name="mla_kv_up")
    cos_m, sin_m = _rope_tables(n_ctx, n_x, MLA_ROPE, MLA_ROPE // 4)
    qf, kf, vf = _mla_prep(qup, kvup, z1, (NA_IN_W + MLA_Q_RANK + MLA_KV_RANK) // LANES, cos_m, sin_m)
    yb1 = _attention(qf, kf, vf, n_heads=MLA_HEADS, q_col=lambda hh: hh, k_col=lambda hh: hh,
                     v_col=lambda hh: hh, dk=MLA_QK_PAD, dv=MLA_V, scale=(MLA_NOPE + MLA_ROPE) ** -0.5,
                     q_row_block0=ctx_blocks, n_q_rows=n_x, n_ctx_q_blocks=0, ctx_len=n_ctx, tq=tm,
                     name="mla_attention")
    hx = _out_proj(ya1, yb1, od_w_out[0], h, mod1, n_ctx_blocks=0, h_row_block0=ctx_blocks, tm=tm)
    return _peer(hx, mod1, norm2_w[1], peer_wq[1], peer_keys[1], peer_u[1], peer_v[1], final_norm_w,
                 n_ctx_blocks=0, n_ctx_rows=0, final_norm=True, tm=tm)
```

```python
import functools
import math

import jax
import jax.numpy as jnp
from jax import lax
from jax.experimental import pallas as pl
from jax.experimental.pallas import tpu as pltpu

F32 = jnp.float32
BF16 = jnp.bfloat16
I32 = jnp.int32

EPS = 1e-6
ROPE_THETA = 10000.0
GRID_W = 64
HEAD_DIM = 128
LANES = 128

GQA_HEADS = 8
GQA_KV_HEADS = 2
GQA_GROUP = GQA_HEADS // GQA_KV_HEADS
GQA_IN_W = (GQA_HEADS + 2 * GQA_KV_HEADS) * HEAD_DIM
HGRN_HEADS = 8
HGRN_CHUNK = 32
NA_HEADS = 8
NA_WIN_R_MAX = 8
NA_WIN_C = 16
NA_IN_W = 3 * NA_HEADS * HEAD_DIM
MLA_HEADS = 8
MLA_Q_RANK = 512
MLA_KV_RANK = 512
MLA_NOPE = 128
MLA_ROPE = 64
MLA_V = 128
MLA_QK_PAD = 256
PEER_HEADS = 8
PEER_N_KEYS = 128
PEER_TOPK = 16
PEER_HALF = 128

VMEM_LIMIT = 56 * 1024 * 1024
NEG_BIG = -1e30


def _cparams(sem):
    return pltpu.CompilerParams(dimension_semantics=sem, vmem_limit_bytes=VMEM_LIMIT)


def _mod_kernel(c_ref, w_ref, b_ref, o_ref):
    c = c_ref[...]
    a = (c * jax.nn.sigmoid(c)).astype(BF16)
    o_ref[...] = jnp.dot(a, w_ref[...].astype(BF16), preferred_element_type=F32) + b_ref[...]


def _modulation(c_all, w, b, tn=1024):
    m, d = c_all.shape
    n = w.shape[1]
    return pl.pallas_call(
        _mod_kernel,
        out_shape=jax.ShapeDtypeStruct((m, n), F32),
        grid=(n // tn,),
        in_specs=[pl.BlockSpec((m, d), lambda j: (0, 0)),
                  pl.BlockSpec((d, tn), lambda j: (0, j)),
                  pl.BlockSpec((1, tn), lambda j: (0, j))],
        out_specs=pl.BlockSpec((m, tn), lambda j: (0, j)),
        compiler_params=_cparams(("parallel",)),
        name="modulation",
    )(c_all, w, b.reshape(1, n))


def _nmm_kernel(*refs, modulate, shift_row, emit_a):
    if modulate:
        h_ref, nw_ref, mod_ref, w_ref = refs[:4]
        rest = refs[4:]
    else:
        h_ref, nw_ref, w_ref = refs[:3]
        mod_ref = None
        rest = refs[3:]
    if emit_a:
        o_ref, a_ref, a_s = rest
    else:
        o_ref, a_s = rest
        a_ref = None

    @pl.when(pl.program_id(2) == 0)
    def _():
        x = h_ref[0].astype(F32)
        y = x * lax.rsqrt(jnp.mean(x * x, axis=-1, keepdims=True) + EPS) * nw_ref[...]
        if modulate:
            shift = mod_ref[0, pl.ds(shift_row, 1), :]
            scale = mod_ref[0, pl.ds(shift_row + 1, 1), :]
            y = y * (1.0 + scale) + shift
        a_s[...] = y.astype(BF16)
        if emit_a:
            a_ref[0] = y.astype(BF16)

    o_ref[0] = jnp.dot(a_s[...], w_ref[...], preferred_element_type=F32).astype(o_ref.dtype)


def _norm_matmul(h, norm_w, w, *, mod=None, shift_row=0, n_ctx_blocks=0, col_block=0, row_block0=0,
                 n_rows=None, tm=256, tn=512, emit_a=False, out_dtype=F32, name="norm_matmul"):
    b, s, _ = h.shape
    k, n = w.shape
    n_rows = s if n_rows is None else n_rows
    nb = b
    modulate = mod is not None
    grid = (b, n_rows // tm, n // tn)
    in_specs = [pl.BlockSpec((1, tm, k), lambda bi, i, j: (bi, i + row_block0, col_block)),
                pl.BlockSpec((1, k), lambda bi, i, j: (0, 0))]
    args = [h, norm_w.reshape(1, k).astype(F32)]
    if modulate:
        in_specs.append(pl.BlockSpec((1, 6, k), lambda bi, i, j: (jnp.where(i < n_ctx_blocks, nb, bi), 0, 0)))
        args.append(mod)
    in_specs.append(pl.BlockSpec((k, tn), lambda bi, i, j: (0, j)))
    args.append(w)
    out_shape = [jax.ShapeDtypeStruct((b, n_rows, n), out_dtype)]
    out_specs = [pl.BlockSpec((1, tm, tn), lambda bi, i, j: (bi, i, j))]
    if emit_a:
        out_shape.append(jax.ShapeDtypeStruct((b, n_rows, k), BF16))
        out_specs.append(pl.BlockSpec((1, tm, k), lambda bi, i, j: (bi, i, 0)))
    res = pl.pallas_call(
        functools.partial(_nmm_kernel, modulate=modulate, shift_row=shift_row, emit_a=emit_a),
        out_shape=out_shape, grid=grid, in_specs=in_specs, out_specs=out_specs,
        scratch_shapes=[pltpu.VMEM((tm, k), BF16)],
        compiler_params=_cparams(("parallel", "parallel", "arbitrary")),
        name=name,
    )(*args)
    return res if emit_a else res[0]


def _outproj_kernel(ya_ref, yb_ref, wa_ref, wb_ref, h_ref, mod_ref, o_ref, *, gate_row):
    acc = jnp.dot(ya_ref[0], wa_ref[...], preferred_element_type=F32)
    acc = acc + jnp.dot(yb_ref[0], wb_ref[...], preferred_element_type=F32)
    gate = mod_ref[0, pl.ds(gate_row, 1), :]
    o_ref[0] = h_ref[0] + gate * acc


def _out_proj(ya, yb, w_out, h, mod, *, n_ctx_blocks, h_row_block0=0, tm=256, tn=512):
    b, n_rows, ka = ya.shape
    kb = yb.shape[2]
    d = w_out.shape[1]
    nb = b
    wa = w_out[:ka].astype(BF16)
    wb = w_out[ka:].astype(BF16)
    return pl.pallas_call(
        functools.partial(_outproj_kernel, gate_row=2),
        out_shape=jax.ShapeDtypeStruct((b, n_rows, d), F32),
        grid=(b, n_rows // tm, d // tn),
        in_specs=[pl.BlockSpec((1, tm, ka), lambda bi, i, j: (bi, i, 0)),
                  pl.BlockSpec((1, tm, kb), lambda bi, i, j: (bi, i, 0)),
                  pl.BlockSpec((ka, tn), lambda bi, i, j: (0, j)),
                  pl.BlockSpec((kb, tn), lambda bi, i, j: (0, j)),
                  pl.BlockSpec((1, tm, tn), lambda bi, i, j: (bi, i + h_row_block0, j)),
                  pl.BlockSpec((1, 6, tn), lambda bi, i, j: (jnp.where(i < n_ctx_blocks, nb, bi), 0, j))],
        out_specs=pl.BlockSpec((1, tm, tn), lambda bi, i, j: (bi, i, j)),
        compiler_params=_cparams(("parallel", "parallel", "parallel")),
        name="out_proj",
    )(ya, yb, wa, wb, h, mod)


def _rope_tables(n_ctx, n_x, width, half):
    t = jnp.arange(n_x)
    lane = jnp.arange(LANES)
    inv = ROPE_THETA ** (-(lane % half).astype(F32) / half)
    pos = jnp.where(lane[None, :] < width // 2, (t // GRID_W)[:, None], (t % GRID_W)[:, None]).astype(F32)
    ang = pos * inv[None, :]
    valid = (lane < width)[None, :]
    cos = jnp.where(valid, jnp.cos(ang), 1.0)
    first = (lane % (2 * half)) < half
    sin = jnp.where(valid, jnp.where(first[None, :], -jnp.sin(ang), jnp.sin(ang)), 0.0)
    cos = jnp.concatenate([jnp.ones((n_ctx, LANES), F32), cos], axis=0)
    sin = jnp.concatenate([jnp.zeros((n_ctx, LANES), F32), sin], axis=0)
    return cos, sin


def _rotate(x, cos, sin, half):
    lane = lax.broadcasted_iota(I32, x.shape, 1)
    first = (lane % (2 * half)) < half
    partner = jnp.where(first, pltpu.roll(x, LANES - half, 1), pltpu.roll(x, half, 1))
    return x * cos + partner * sin


def _gqa_prep_kernel(z_ref, qw_ref, kw_ref, cos_ref, sin_ref, o_ref):
    c = pl.program_id(2)
    x = z_ref[0]

    def normed(w):
        y = x * lax.rsqrt(jnp.mean(x * x, axis=-1, keepdims=True) + EPS) * w
        return _rotate(y, cos_ref[...], sin_ref[...], HEAD_DIM // 4)

    @pl.when(c < GQA_HEADS)
    def _():
        o_ref[0] = normed(qw_ref[...]).astype(BF16)

    @pl.when(jnp.logical_and(c >= GQA_HEADS, c < GQA_HEADS + GQA_KV_HEADS))
    def _():
        o_ref[0] = normed(kw_ref[...]).astype(BF16)

    @pl.when(c >= GQA_HEADS + GQA_KV_HEADS)
    def _():
        o_ref[0] = x.astype(BF16)


def _gqa_prep(z, q_norm_w, k_norm_w, cos, sin, ts=256):
    b, s, _ = z.shape
    ncol = GQA_IN_W // HEAD_DIM
    return pl.pallas_call(
        _gqa_prep_kernel,
        out_shape=jax.ShapeDtypeStruct((b, s, GQA_IN_W), BF16),
        grid=(b, s // ts, ncol),
        in_specs=[pl.BlockSpec((1, ts, HEAD_DIM), lambda bi, i, c: (bi, i, c)),
                  pl.BlockSpec((1, HEAD_DIM), lambda bi, i, c: (0, 0)),
                  pl.BlockSpec((1, HEAD_DIM), lambda bi, i, c: (0, 0)),
                  pl.BlockSpec((ts, LANES), lambda bi, i, c: (i, 0)),
                  pl.BlockSpec((ts, LANES), lambda bi, i, c: (i, 0))],
        out_specs=pl.BlockSpec((1, ts, HEAD_DIM), lambda bi, i, c: (bi, i, c)),
        compiler_params=_cparams(("parallel", "parallel", "parallel")),
        name="gqa_prep",
    )(z, q_norm_w.reshape(1, HEAD_DIM), k_norm_w.reshape(1, HEAD_DIM), cos, sin)


def _attn_kernel(q_ref, k_ref, v_ref, o_ref, *, scale, n_ctx_q_blocks, ctx_len, q_axis):
    q = q_ref[0]

    def attend(n_keys):
        k = k_ref[0, pl.ds(0, n_keys), :]
        v = v_ref[0, pl.ds(0, n_keys), :]
        s = lax.dot_general(q, k, (((1,), (1,)), ((), ())), preferred_element_type=F32) * scale
        m = jnp.max(s, axis=-1, keepdims=True)
        p = jnp.exp(s - m)
        l = jnp.sum(p, axis=-1, keepdims=True)
        o = jnp.dot(p.astype(BF16), v, preferred_element_type=F32)
        o_ref[0] = (o / l).astype(o_ref.dtype)

    n_all = k_ref.shape[1]
    if n_ctx_q_blocks == 0:
        attend(n_all)
    else:
        qi = pl.program_id(q_axis)

        @pl.when(qi < n_ctx_q_blocks)
        def _():
            attend(ctx_len)

        @pl.when(qi >= n_ctx_q_blocks)
        def _():
            attend(n_all)


def _attention(q_arr, k_arr, v_arr, *, n_heads, q_col, k_col, v_col, dk, dv, scale, q_row_block0,
               n_q_rows, n_ctx_q_blocks, ctx_len, tq=256, name="attention"):
    b, s, _ = k_arr.shape
    return pl.pallas_call(
        functools.partial(_attn_kernel, scale=scale, n_ctx_q_blocks=n_ctx_q_blocks, ctx_len=ctx_len, q_axis=2),
        out_shape=jax.ShapeDtypeStruct((b, n_q_rows, n_heads * dv), BF16),
        grid=(b, n_heads, n_q_rows // tq),
        in_specs=[pl.BlockSpec((1, tq, dk), lambda bi, h, i: (bi, i + q_row_block0, q_col(h))),
                  pl.BlockSpec((1, s, dk), lambda bi, h, i: (bi, 0, k_col(h))),
                  pl.BlockSpec((1, s, dv), lambda bi, h, i: (bi, 0, v_col(h)))],
        out_specs=pl.BlockSpec((1, tq, dv), lambda bi, h, i: (bi, i, h)),
        compiler_params=_cparams(("parallel", "parallel", "parallel")),
        name=name,
    )(q_arr, k_arr, v_arr)


def _split3(x):
    hi = x.astype(BF16)
    r1 = x - hi.astype(F32)
    mid = r1.astype(BF16)
    lo = (r1 - mid.astype(F32)).astype(BF16)
    return hi, mid, lo


def _exact_dot(a_bf16, x):
    hi, mid, lo = _split3(x)
    return (jnp.dot(a_bf16, hi, preferred_element_type=F32)
            + jnp.dot(a_bf16, mid, preferred_element_type=F32)
            + jnp.dot(a_bf16, lo, preferred_element_type=F32))


def _hgrn_kernel(q_ref, ff_ref, fb_ref, i_ref, g_ref, lbf_ref, lbb_ref, ow_ref, o_ref,
                 of_s, cum_s, kk_s, st_s, *, n_ctx_chunks, n_chunks):
    c_len = HGRN_CHUNK
    row = lax.broadcasted_iota(I32, (c_len, c_len), 0)
    col = lax.broadcasted_iota(I32, (c_len, c_len), 1)
    tri_f = (col <= row).astype(BF16)
    tri_b = (col >= row).astype(BF16)
    t_idx = lax.broadcasted_iota(I32, (c_len, HEAD_DIM), 0)

    def chunk(r0, f_ref, lb_ref, forward):
        rows = pl.ds(r0, c_len)
        q = q_ref[0, rows, :]
        v = i_ref[0, rows, :]
        lb = lb_ref[0]
        sg = jax.nn.sigmoid(f_ref[0, rows, :])
        k = (1.0 - lb) * (1.0 - sg)
        g = jnp.log(lb + (1.0 - lb) * sg)
        cum = _exact_dot(tri_f if forward else tri_b, g)
        total = jnp.sum(g, axis=0, keepdims=True)
        cum_s[...] = cum
        kk_s[...] = k
        o = jnp.zeros((c_len, HEAD_DIM), F32)
        for s in range(c_len):
            cs = cum_s[pl.ds(s, 1), :]
            ks = kk_s[pl.ds(s, 1), :]
            keep = (t_idx >= s) if forward else (t_idx <= s)
            w = q * ks * jnp.exp(jnp.where(keep, cum - cs, -jnp.inf))
            a = jnp.sum(w, axis=-1, keepdims=True)
            o = o + a * v[s:s + 1, :]
        st = st_s[...]
        qd = (q * jnp.exp(cum)).astype(BF16)
        o = o + lax.dot_general(qd, st.astype(BF16), (((1,), (1,)), ((), ())), preferred_element_type=F32)
        kd = (k * jnp.exp(total - cum)).astype(BF16)
        st_s[...] = st * jnp.exp(total) + lax.dot_general(v.astype(BF16), kd, (((0,), (0,)), ((), ())),
                                                         preferred_element_type=F32)
        return o

    st_s[...] = jnp.zeros_like(st_s)

    def fwd_body(c, carry):
        r0 = pl.multiple_of(c * c_len, c_len)
        of_s[pl.ds(r0, c_len), :] = chunk(r0, ff_ref, lbf_ref, True)
        return carry

    lax.fori_loop(0, n_chunks, fwd_body, 0)

    st_s[...] = jnp.zeros_like(st_s)

    def bwd_body(j, carry):
        c = jnp.where(j < n_ctx_chunks, n_ctx_chunks - 1 - j, n_chunks - 1 - (j - n_ctx_chunks))
        r0 = pl.multiple_of(c * c_len, c_len)
        rows = pl.ds(r0, c_len)
        o = of_s[rows, :] + chunk(r0, fb_ref, lbb_ref, False)
        y = o * lax.rsqrt(jnp.mean(o * o, axis=-1, keepdims=True) + EPS) * ow_ref[0]
        gate = g_ref[0, rows, :]
        o_ref[0, rows, :] = (y * (gate * jax.nn.sigmoid(gate))).astype(o_ref.dtype)
        return carry

    lax.fori_loop(0, n_chunks, bwd_body, 0)


def _hgrn(z, col0, lb_fw, lb_bw, o_norm_w, ctx_len):
    b, s, _ = z.shape
    nh = HGRN_HEADS
    blk = lambda grp: pl.BlockSpec((1, s, HEAD_DIM), lambda bi, h: (bi, 0, col0 + grp * nh + h))
    vec = pl.BlockSpec((1, 1, HEAD_DIM), lambda bi, h: (h, 0, 0))
    return pl.pallas_call(
        functools.partial(_hgrn_kernel, n_ctx_chunks=ctx_len // HGRN_CHUNK, n_chunks=s // HGRN_CHUNK),
        out_shape=jax.ShapeDtypeStruct((b, s, nh * HEAD_DIM), BF16),
        grid=(b, nh),
        in_specs=[blk(0), blk(1), blk(2), blk(3), blk(4), vec, vec, vec],
        out_specs=pl.BlockSpec((1, s, HEAD_DIM), lambda bi, h: (bi, 0, h)),
        scratch_shapes=[pltpu.VMEM((s, HEAD_DIM), F32),
                        pltpu.VMEM((HGRN_CHUNK, HEAD_DIM), F32),
                        pltpu.VMEM((HGRN_CHUNK, HEAD_DIM), F32),
                        pltpu.VMEM((HEAD_DIM, HEAD_DIM), F32)],
        compiler_params=_cparams(("parallel", "parallel")),
        name="hgrn",
    )(z, z, z, z, z, lb_fw.reshape(nh, 1, HEAD_DIM), lb_bw.reshape(nh, 1, HEAD_DIM),
      o_norm_w.reshape(nh, 1, HEAD_DIM))


def _na_kernel(q_ref, k_ref, v_ref, bias_ref, o_ref, *, scale, ctx_len, rows, win_r):
    r = pl.program_id(2)
    r0 = jnp.clip(r - win_r // 2, 0, rows - win_r)
    start = pl.multiple_of(ctx_len + r0 * GRID_W, GRID_W)
    band = pl.ds(start, win_r * GRID_W)
    q = q_ref[0].astype(BF16)
    kc = k_ref[0, pl.ds(0, ctx_len), :].astype(BF16)
    vc = v_ref[0, pl.ds(0, ctx_len), :].astype(BF16)
    kb = k_ref[0, band, :].astype(BF16)
    vb = v_ref[0, band, :].astype(BF16)
    nt = (((1,), (1,)), ((), ()))
    s_ctx = lax.dot_general(q, kc, nt, preferred_element_type=F32) * scale
    s_loc = lax.dot_general(q, kb, nt, preferred_element_type=F32) * scale + bias_ref[0, 0]
    m = jnp.maximum(jnp.max(s_ctx, axis=-1, keepdims=True), jnp.max(s_loc, axis=-1, keepdims=True))
    p_ctx = jnp.exp(s_ctx - m)
    p_loc = jnp.exp(s_loc - m)
    l = jnp.sum(p_ctx, axis=-1, keepdims=True) + jnp.sum(p_loc, axis=-1, keepdims=True)
    o = jnp.dot(p_ctx.astype(BF16), vc, preferred_element_type=F32)
    o = o + jnp.dot(p_loc.astype(BF16), vb, preferred_element_type=F32)
    o_ref[0] = (o / l).astype(o_ref.dtype)


def _na_bias_table(rpb, rows, win_r):
    cols = jnp.arange(GRID_W)
    col_start = jnp.clip(cols - NA_WIN_C // 2, 0, GRID_W - NA_WIN_C)
    in_win = (cols[None, :] >= col_start[:, None]) & (cols[None, :] < col_start[:, None] + NA_WIN_C)
    dc_idx = jnp.clip(cols[None, :] - cols[:, None] + NA_WIN_C - 1, 0, 2 * NA_WIN_C - 2)
    n_off = win_r
    off = jnp.arange(n_off)
    dr_idx = jnp.arange(win_r)[None, :] - off[:, None] + NA_WIN_R_MAX - 1
    dr_ok = (dr_idx >= 0) & (dr_idx <= 2 * NA_WIN_R_MAX - 2)
    bias = rpb[:, jnp.clip(dr_idx, 0, 2 * NA_WIN_R_MAX - 2)][:, :, :, dc_idx]
    keep = in_win[None, None, None] & dr_ok[None, :, :, None, None]
    bias = jnp.where(keep, bias, NEG_BIG)
    bias = bias.transpose(0, 1, 3, 2, 4).reshape(rpb.shape[0], n_off, GRID_W, win_r * GRID_W)
    return bias.astype(F32)


def _na(z, rpb, ctx_len, n_x):
    b, s, _ = z.shape
    rows = n_x // GRID_W
    win_r = min(NA_WIN_R_MAX, rows)
    bias = _na_bias_table(rpb, rows, win_r)
    nh = NA_HEADS
    ctx_blocks = ctx_len // GRID_W

    def off(r):
        return r - jnp.clip(r - win_r // 2, 0, rows - win_r)

    return pl.pallas_call(
        functools.partial(_na_kernel, scale=HEAD_DIM ** -0.5, ctx_len=ctx_len, rows=rows, win_r=win_r),
        out_shape=jax.ShapeDtypeStruct((b, n_x, nh * HEAD_DIM), BF16),
        grid=(b, nh, rows),
        in_specs=[pl.BlockSpec((1, GRID_W, HEAD_DIM), lambda bi, h, r: (bi, r + ctx_blocks, h)),
                  pl.BlockSpec((1, s, HEAD_DIM), lambda bi, h, r: (bi, 0, nh + h)),
                  pl.BlockSpec((1, s, HEAD_DIM), lambda bi, h, r: (bi, 0, 2 * nh + h)),
                  pl.BlockSpec((1, 1, GRID_W, win_r * GRID_W), lambda bi, h, r: (h, off(r), 0, 0))],
        out_specs=pl.BlockSpec((1, GRID_W, HEAD_DIM), lambda bi, h, r: (bi, r, h)),
        compiler_params=_cparams(("parallel", "parallel", "parallel")),
        name="na",
    )(z, z, z, bias)


def _mla_prep_kernel(qup_ref, kvup_ref, kr_ref, cos_ref, sin_ref, qf_ref, kf_ref, vf_ref):
    cos = cos_ref[...]
    sin = sin_ref[...]
    qf_ref[0, :, :LANES] = qup_ref[0, :, :LANES].astype(BF16)
    qf_ref[0, :, LANES:] = _rotate(qup_ref[0, :, LANES:], cos, sin, MLA_ROPE // 4).astype(BF16)
    kf_ref[0, :, :LANES] = kvup_ref[0, :, :LANES].astype(BF16)
    kf_ref[0, :, LANES:] = _rotate(kr_ref[0], cos, sin, MLA_ROPE // 4).astype(BF16)
    vf_ref[0] = kvup_ref[0, :, LANES:].astype(BF16)


def _mla_prep(qup, kvup, z, kr_col, cos, sin, ts=256):
    b, s, _ = qup.shape
    nh = MLA_HEADS
    return pl.pallas_call(
        _mla_prep_kernel,
        out_shape=[jax.ShapeDtypeStruct((b, s, nh * MLA_QK_PAD), BF16),
                   jax.ShapeDtypeStruct((b, s, nh * MLA_QK_PAD), BF16),
                   jax.ShapeDtypeStruct((b, s, nh * MLA_V), BF16)],
        grid=(b, s // ts, nh),
        in_specs=[pl.BlockSpec((1, ts, MLA_QK_PAD), lambda bi, i, h: (bi, i, h)),
                  pl.BlockSpec((1, ts, MLA_NOPE + MLA_V), lambda bi, i, h: (bi, i, h)),
                  pl.BlockSpec((1, ts, LANES), lambda bi, i, h: (bi, i, kr_col)),
                  pl.BlockSpec((ts, LANES), lambda bi, i, h: (i, 0)),
                  pl.BlockSpec((ts, LANES), lambda bi, i, h: (i, 0))],
        out_specs=[pl.BlockSpec((1, ts, MLA_QK_PAD), lambda bi, i, h: (bi, i, h)),
                   pl.BlockSpec((1, ts, MLA_QK_PAD), lambda bi, i, h: (bi, i, h)),
                   pl.BlockSpec((1, ts, MLA_V), lambda bi, i, h: (bi, i, h))],
        compiler_params=_cparams(("parallel", "parallel", "parallel")),
        name="mla_prep",
    )(qup, kvup, z, cos, sin)


def _topk_rows(x, k, payload=None):
    n, width = x.shape
    row = lax.broadcasted_iota(I32, (n, width), 0).astype(F32)
    out_row = lax.broadcasted_iota(I32, (k, width), 0)
    vals = jnp.zeros((k, width), F32)
    poss = jnp.zeros((k, width), F32)
    pays = jnp.zeros((k, width), F32)
    for r in range(k):
        m = jnp.max(x, axis=0, keepdims=True)
        pos = jnp.min(jnp.where(x == m, row, float(n)), axis=0, keepdims=True)
        hit = row == pos
        vals = jnp.where(out_row == r, m, vals)
        poss = jnp.where(out_row == r, pos, poss)
        if payload is not None:
            pay = jnp.max(jnp.where(hit, payload, -1.0), axis=0, keepdims=True)
            pays = jnp.where(out_row == r, pay, pays)
        x = jnp.where(hit, -jnp.inf, x)
    return vals, poss, pays


def _peer_route_kernel(q_ref, keys_ref, gate_ref, eidx_ref):
    kk = PEER_TOPK
    nt = (((1,), (1,)), ((), ()))
    for h in range(PEER_HEADS):
        sv, si = [], []
        for p in range(2):
            j = 2 * h + p
            qj = q_ref[:, j * PEER_HALF:(j + 1) * PEER_HALF].astype(BF16)
            s = lax.dot_general(keys_ref[j], qj, nt, preferred_element_type=F32)
            v, pos, _ = _topk_rows(s, kk)
            sv.append(v)
            si.append(pos)
        cand = jnp.concatenate([sv[0][a:a + 1, :] + sv[1] for a in range(kk)], axis=0)
        cidx = jnp.concatenate([si[0][a:a + 1, :] * float(PEER_N_KEYS) + si[1] for a in range(kk)], axis=0)
        top_s, _, eidx = _topk_rows(cand, kk, payload=cidx)
        e = jnp.exp(top_s - top_s[0:1, :])
        gate_ref[h] = e / jnp.sum(e, axis=0, keepdims=True)
        eidx_ref[h] = eidx.astype(I32)


def _peer_route(q, keys, tm=256):
    t = q.shape[0]
    nh, kk = PEER_HEADS, PEER_TOPK
    gate, eidx = pl.pallas_call(
        _peer_route_kernel,
        out_shape=[jax.ShapeDtypeStruct((nh, kk, t), F32), jax.ShapeDtypeStruct((nh, kk, t), I32)],
        grid=(t // tm,),
        in_specs=[pl.BlockSpec((tm, q.shape[1]), lambda i: (i, 0)),
                  pl.BlockSpec(keys.shape, lambda i: (0, 0, 0))],
        out_specs=[pl.BlockSpec((nh, kk, tm), lambda i: (0, 0, i)),
                   pl.BlockSpec((nh, kk, tm), lambda i: (0, 0, i))],
        compiler_params=_cparams(("parallel",)),
        name="peer_route",
    )(q, keys)
    return gate.reshape(nh * kk, t).T, eidx.reshape(nh * kk, t).T


def _gelu_tanh(x):
    return 0.5 * x * (1.0 + jnp.tanh(math.sqrt(2.0 / math.pi) * (x + 0.044715 * x * x * x)))


def _pack_tables(u, v):
    ub = lax.bitcast_convert_type(u.astype(BF16), jnp.uint16).astype(jnp.uint32)
    vb = lax.bitcast_convert_type(v.astype(BF16), jnp.uint16).astype(jnp.uint32)
    return (ub | (vb << 16))[:, None, :]


SUBLANES = 8


def _peer_expert_kernel(idx_cur, idx_nxt, f_ref, gate_ref, h_ref, mod_ref, fw_ref, uv_hbm, o_ref,
                        buf, sem, *, tb, n_sel, final_norm):
    i = pl.program_id(0)
    n = pl.num_programs(0)
    slot = i % 2
    groups = n_sel // SUBLANES

    def issue(idx_ref, s):
        def grp(g, carry):
            base = g * SUBLANES
            for k in range(SUBLANES):
                row = idx_ref[base + k]
                pltpu.make_async_copy(uv_hbm.at[row], buf.at[s, g, pl.ds(k, 1), :],
                                      sem.at[s]).start(priority=k % 2)
            return carry
        lax.fori_loop(0, tb * groups, grp, 0)

    @pl.when(i == 0)
    def _():
        issue(idx_cur, 0)

    @pl.when(i + 1 < n)
    def _():
        issue(idx_nxt, 1 - slot)

    pltpu.make_async_copy(buf.at[slot], buf.at[slot], sem.at[slot]).wait()

    x = f_ref[...]
    tok_row = lax.broadcasted_iota(I32, (tb, n_sel), 0)
    nt = (((1,), (1,)), ((), ()))
    d = buf.shape[-1]

    def words(t):
        return buf[slot, pl.ds(pl.multiple_of(t * groups, groups), groups)].reshape(n_sel, d)

    def score(t, hh):
        u_t = lax.bitcast_convert_type(words(t) << 16, F32).astype(BF16)
        s = lax.dot_general(x, u_t, nt, preferred_element_type=F32)
        return jnp.where(tok_row == t, s, hh)

    hh = lax.fori_loop(0, tb, score, jnp.zeros((tb, n_sel), F32), unroll=2)
    act = gate_ref[...] * _gelu_tanh(hh)

    def mix(t, acc):
        v_t = lax.bitcast_convert_type(words(t) & jnp.uint32(0xFFFF0000), F32).astype(BF16)
        wm = jnp.where(tok_row == t, act, 0.0).astype(BF16)
        return acc + jnp.dot(wm, v_t, preferred_element_type=F32)

    y = lax.fori_loop(0, tb, mix, jnp.zeros(o_ref.shape, F32), unroll=2)
    out = h_ref[...] + mod_ref[0, pl.ds(5, 1), :] * y
    if final_norm:
        out = out * lax.rsqrt(jnp.mean(out * out, axis=-1, keepdims=True) + EPS) * fw_ref[...]
    o_ref[...] = out


def _peer_experts(f, gate, eidx, h, mod, uv, final_w, *, rows_per_batch, n_ctx_rows, final_norm, tb=16):
    t, d = f.shape
    n_sel = gate.shape[1]
    nb = mod.shape[0] - 1
    bpb = rows_per_batch // tb
    ctx_blocks = n_ctx_rows // tb
    n_blocks = t // tb
    eflat = eidx.reshape(t * n_sel)

    def mod_map(i):
        return (jnp.where(i % bpb < ctx_blocks, nb, i // bpb), 0, 0)

    smem = pltpu.SMEM
    return pl.pallas_call(
        functools.partial(_peer_expert_kernel, tb=tb, n_sel=n_sel, final_norm=final_norm),
        out_shape=jax.ShapeDtypeStruct((t, d), F32),
        grid=(n_blocks,),
        in_specs=[pl.BlockSpec((tb * n_sel,), lambda i: (i,), memory_space=smem),
                  pl.BlockSpec((tb * n_sel,), lambda i: (jnp.minimum(i + 1, n_blocks - 1),), memory_space=smem),
                  pl.BlockSpec((tb, d), lambda i: (i, 0)),
                  pl.BlockSpec((tb, n_sel), lambda i: (i, 0)),
                  pl.BlockSpec((tb, d), lambda i: (i, 0)),
                  pl.BlockSpec((1, 6, d), mod_map),
                  pl.BlockSpec((1, d), lambda i: (0, 0)),
                  pl.BlockSpec(memory_space=pl.ANY)],
        out_specs=pl.BlockSpec((tb, d), lambda i: (i, 0)),
        scratch_shapes=[pltpu.VMEM((2, tb * n_sel // SUBLANES, SUBLANES, d), jnp.uint32),
                        pltpu.SemaphoreType.DMA((2,))],
        compiler_params=_cparams(("arbitrary",)),
        name="peer_experts",
    )(eflat, eflat, f, gate, h, mod, final_w.reshape(1, d).astype(F32), uv)


def _peer(h, mod, norm_w, wq, keys, u, v, final_w, *, n_ctx_blocks, n_ctx_rows, final_norm, tm=256):
    b, r, d = h.shape
    q, f = _norm_matmul(h, norm_w, wq.astype(BF16), mod=mod, shift_row=3, n_ctx_blocks=n_ctx_blocks,
                        tm=tm, emit_a=True, name="peer_query")
    keys2 = keys.reshape(PEER_HEADS * 2, PEER_N_KEYS, PEER_HALF).astype(BF16)
    gate, eidx = _peer_route(q.reshape(b * r, -1), keys2, tm=tm)
    out = _peer_experts(f.reshape(b * r, d), gate, eidx, h.reshape(b * r, d), mod, _pack_tables(u, v), final_w,
                        rows_per_batch=r, n_ctx_rows=n_ctx_rows, final_norm=final_norm)
    return out.reshape(b, r, d)


def kernel(x, c, ctx, c_ctx, w_mod, b_mod, norm1_w, norm2_w, ev_w_in, ev_w_out, gqa_q_norm, gqa_k_norm, hgrn_lb, hgrn_o_norm, od_w_in, od_w_out, na_rpb, mla_q_norm, mla_w_q_up, mla_kv_norm, mla_w_kv_up, peer_wq, peer_keys, peer_u, peer_v, final_norm_w):
    b, n_x, d = x.shape
    n_ctx = ctx.shape[1]
    depth = w_mod.shape[0]
    tm = 256
    assert n_ctx % tm == 0 and n_x % tm == 0 and depth == 2
    ctx_blocks = n_ctx // tm

    h = jnp.concatenate([ctx, x], axis=1)
    c_all = jnp.concatenate([c, c_ctx[None, :]], axis=0)
    m_pad = -(-c_all.shape[0] // 8) * 8
    c_pad = jnp.pad(c_all, ((0, m_pad - c_all.shape[0]), (0, 0)))
    lb_table = jnp.cumsum(jax.nn.softmax(hgrn_lb.astype(F32), axis=1), axis=1)

    mod0 = _modulation(c_pad, w_mod[0], b_mod[0])[:b + 1].reshape(b + 1, 6, d)
    z = _norm_matmul(h, norm1_w[0], ev_w_in[0].astype(BF16), mod=mod0, shift_row=0, n_ctx_blocks=ctx_blocks,
                     tm=tm, name="in_proj0")
    cos_g, sin_g = _rope_tables(n_ctx, n_x, HEAD_DIM, HEAD_DIM // 4)
    qkv = _gqa_prep(z, gqa_q_norm[0], gqa_k_norm[0], cos_g, sin_g)
    ya = _attention(qkv, qkv, qkv, n_heads=GQA_HEADS, q_col=lambda hh: hh,
                    k_col=lambda hh: GQA_HEADS + hh // GQA_GROUP,
                    v_col=lambda hh: GQA_HEADS + GQA_KV_HEADS + hh // GQA_GROUP,
                    dk=HEAD_DIM, dv=HEAD_DIM, scale=HEAD_DIM ** -0.5, q_row_block0=0, n_q_rows=n_ctx + n_x,
                    n_ctx_q_blocks=ctx_blocks, ctx_len=n_ctx, tq=tm, name="gqa_attention")
    yb = _hgrn(z, GQA_IN_W // HEAD_DIM, lb_table[0, 0], lb_table[1, 0], hgrn_o_norm[0], n_ctx)
    h = _out_proj(ya, yb, ev_w_out[0], h, mod0, n_ctx_blocks=ctx_blocks, tm=tm)
    h = _peer(h, mod0, norm2_w[0], peer_wq[0], peer_keys[0], peer_u[0], peer_v[0], final_norm_w,
              n_ctx_blocks=ctx_blocks, n_ctx_rows=n_ctx, final_norm=False, tm=tm)

    mod1 = _modulation(c_pad, w_mod[1], b_mod[1])[:b + 1].reshape(b + 1, 6, d)
    odd_w = od_w_in[0]
    kr_pad = LANES - MLA_ROPE
    odd_w = jnp.pad(odd_w, ((0, 0), (0, kr_pad))).astype(BF16)
    z1 = _norm_matmul(h, norm1_w[1], odd_w, mod=mod1, shift_row=0, n_ctx_blocks=ctx_blocks, tm=tm,
                      tn=odd_w.shape[1] // 3, name="in_proj1")
    ya1 = _na(z1, na_rpb[0], n_ctx, n_x)

    qw = mla_w_q_up[0].reshape(MLA_Q_RANK, MLA_HEADS, MLA_NOPE + MLA_ROPE)
    qw = jnp.pad(qw, ((0, 0), (0, 0), (0, MLA_QK_PAD - MLA_NOPE - MLA_ROPE)))
    qw = qw.reshape(MLA_Q_RANK, MLA_HEADS * MLA_QK_PAD).astype(BF16)
    cq_col = NA_IN_W // MLA_Q_RANK
    qup = _norm_matmul(z1, mla_q_norm[0], qw, col_block=cq_col, tm=tm, name="mla_q_up")
    kvup = _norm_matmul(z1, mla_kv_norm[0], mla_w_kv_up[0].astype(BF16), col_block=cq_col + 1, tm=tm,
                        name="mla_kv_up")
    cos_m, sin_m = _rope_tables(n_ctx, n_x, MLA_ROPE, MLA_ROPE // 4)
    qf, kf, vf = _mla_prep(qup, kvup, z1, (NA_IN_W + MLA_Q_RANK + MLA_KV_RANK) // LANES, cos_m, sin_m)
    yb1 = _attention(qf, kf, vf, n_heads=MLA_HEADS, q_col=lambda hh: hh, k_col=lambda hh: hh,
                     v_col=lambda hh: hh, dk=MLA_QK_PAD, dv=MLA_V, scale=(MLA_NOPE + MLA_ROPE) ** -0.5,
                     q_row_block0=ctx_blocks, n_q_rows=n_x, n_ctx_q_blocks=0, ctx_len=n_ctx, tq=tm,
                     name="mla_attention")
    hx = _out_proj(ya1, yb1, od_w_out[0], h, mod1, n_ctx_blocks=0, h_row_block0=ctx_blocks, tm=tm)
    return _peer(hx, mod1, norm2_w[1], peer_wq[1], peer_keys[1], peer_u[1], peer_v[1], final_norm_w,
                 n_ctx_blocks=0, n_ctx_rows=0, final_norm=True, tm=tm)
```

```python
import functools
import math

import jax
import jax.numpy as jnp
from jax import lax
from jax.experimental import pallas as pl
from jax.experimental.pallas import tpu as pltpu

F32 = jnp.float32
BF16 = jnp.bfloat16
I32 = jnp.int32

EPS = 1e-6
ROPE_THETA = 10000.0
GRID_W = 64
HEAD_DIM = 128
LANES = 128

GQA_HEADS = 8
GQA_KV_HEADS = 2
GQA_GROUP = GQA_HEADS // GQA_KV_HEADS
GQA_IN_W = (GQA_HEADS + 2 * GQA_KV_HEADS) * HEAD_DIM
HGRN_HEADS = 8
HGRN_CHUNK = 32
NA_HEADS = 8
NA_WIN_R_MAX = 8
NA_WIN_C = 16
NA_IN_W = 3 * NA_HEADS * HEAD_DIM
MLA_HEADS = 8
MLA_Q_RANK = 512
MLA_KV_RANK = 512
MLA_NOPE = 128
MLA_ROPE = 64
MLA_V = 128
MLA_QK_PAD = 256
PEER_HEADS = 8
PEER_N_KEYS = 128
PEER_TOPK = 16
PEER_HALF = 128

VMEM_LIMIT = 56 * 1024 * 1024
NEG_BIG = -1e30


def _cparams(sem):
    return pltpu.CompilerParams(dimension_semantics=sem, vmem_limit_bytes=VMEM_LIMIT)


def _mod_kernel(c_ref, w_ref, b_ref, o_ref):
    c = c_ref[...]
    a = (c * jax.nn.sigmoid(c)).astype(BF16)
    o_ref[...] = jnp.dot(a, w_ref[...].astype(BF16), preferred_element_type=F32) + b_ref[...]


def _modulation(c_all, w, b, tn=1024):
    m, d = c_all.shape
    n = w.shape[1]
    return pl.pallas_call(
        _mod_kernel,
        out_shape=jax.ShapeDtypeStruct((m, n), F32),
        grid=(n // tn,),
        in_specs=[pl.BlockSpec((m, d), lambda j: (0, 0)),
                  pl.BlockSpec((d, tn), lambda j: (0, j)),
                  pl.BlockSpec((1, tn), lambda j: (0, j))],
        out_specs=pl.BlockSpec((m, tn), lambda j: (0, j)),
        compiler_params=_cparams(("parallel",)),
        name="modulation",
    )(c_all, w, b.reshape(1, n))


def _nmm_kernel(*refs, modulate, shift_row, emit_a):
    if modulate:
        h_ref, nw_ref, mod_ref, w_ref = refs[:4]
        rest = refs[4:]
    else:
        h_ref, nw_ref, w_ref = refs[:3]
        mod_ref = None
        rest = refs[3:]
    if emit_a:
        o_ref, a_ref, a_s = rest
    else:
        o_ref, a_s = rest
        a_ref = None

    @pl.when(pl.program_id(2) == 0)
    def _():
        x = h_ref[0].astype(F32)
        y = x * lax.rsqrt(jnp.mean(x * x, axis=-1, keepdims=True) + EPS) * nw_ref[...]
        if modulate:
            shift = mod_ref[0, pl.ds(shift_row, 1), :]
            scale = mod_ref[0, pl.ds(shift_row + 1, 1), :]
            y = y * (1.0 + scale) + shift
        a_s[...] = y.astype(BF16)
        if emit_a:
            a_ref[0] = y.astype(BF16)

    o_ref[0] = jnp.dot(a_s[...], w_ref[...], preferred_element_type=F32).astype(o_ref.dtype)


def _norm_matmul(h, norm_w, w, *, mod=None, shift_row=0, n_ctx_blocks=0, col_block=0, row_block0=0,
                 n_rows=None, tm=256, tn=512, emit_a=False, out_dtype=F32, name="norm_matmul"):
    b, s, _ = h.shape
    k, n = w.shape
    n_rows = s if n_rows is None else n_rows
    nb = b
    modulate = mod is not None
    grid = (b, n_rows // tm, n // tn)
    in_specs = [pl.BlockSpec((1, tm, k), lambda bi, i, j: (bi, i + row_block0, col_block)),
                pl.BlockSpec((1, k), lambda bi, i, j: (0, 0))]
    args = [h, norm_w.reshape(1, k).astype(F32)]
    if modulate:
        in_specs.append(pl.BlockSpec((1, 6, k), lambda bi, i, j: (jnp.where(i < n_ctx_blocks, nb, bi), 0, 0)))
        args.append(mod)
    in_specs.append(pl.BlockSpec((k, tn), lambda bi, i, j: (0, j)))
    args.append(w)
    out_shape = [jax.ShapeDtypeStruct((b, n_rows, n), out_dtype)]
    out_specs = [pl.BlockSpec((1, tm, tn), lambda bi, i, j: (bi, i, j))]
    if emit_a:
        out_shape.append(jax.ShapeDtypeStruct((b, n_rows, k), BF16))
        out_specs.append(pl.BlockSpec((1, tm, k), lambda bi, i, j: (bi, i, 0)))
    res = pl.pallas_call(
        functools.partial(_nmm_kernel, modulate=modulate, shift_row=shift_row, emit_a=emit_a),
        out_shape=out_shape, grid=grid, in_specs=in_specs, out_specs=out_specs,
        scratch_shapes=[pltpu.VMEM((tm, k), BF16)],
        compiler_params=_cparams(("parallel", "parallel", "arbitrary")),
        name=name,
    )(*args)
    return res if emit_a else res[0]


def _outproj_kernel(ya_ref, yb_ref, wa_ref, wb_ref, h_ref, mod_ref, o_ref, *, gate_row):
    acc = jnp.dot(ya_ref[0], wa_ref[...], preferred_element_type=F32)
    acc = acc + jnp.dot(yb_ref[0], wb_ref[...], preferred_element_type=F32)
    gate = mod_ref[0, pl.ds(gate_row, 1), :]
    o_ref[0] = h_ref[0] + gate * acc


def _out_proj(ya, yb, w_out, h, mod, *, n_ctx_blocks, h_row_block0=0, tm=256, tn=512):
    b, n_rows, ka = ya.shape
    kb = yb.shape[2]
    d = w_out.shape[1]
    nb = b
    wa = w_out[:ka].astype(BF16)
    wb = w_out[ka:].astype(BF16)
    return pl.pallas_call(
        functools.partial(_outproj_kernel, gate_row=2),
        out_shape=jax.ShapeDtypeStruct((b, n_rows, d), F32),
        grid=(b, n_rows // tm, d // tn),
        in_specs=[pl.BlockSpec((1, tm, ka), lambda bi, i, j: (bi, i, 0)),
                  pl.BlockSpec((1, tm, kb), lambda bi, i, j: (bi, i, 0)),
                  pl.BlockSpec((ka, tn), lambda bi, i, j: (0, j)),
                  pl.BlockSpec((kb, tn), lambda bi, i, j: (0, j)),
                  pl.BlockSpec((1, tm, tn), lambda bi, i, j: (bi, i + h_row_block0, j)),
                  pl.BlockSpec((1, 6, tn), lambda bi, i, j: (jnp.where(i < n_ctx_blocks, nb, bi), 0, j))],
        out_specs=pl.BlockSpec((1, tm, tn), lambda bi, i, j: (bi, i, j)),
        compiler_params=_cparams(("parallel", "parallel", "parallel")),
        name="out_proj",
    )(ya, yb, wa, wb, h, mod)


def _rope_tables(n_ctx, n_x, width, half):
    t = jnp.arange(n_x)
    lane = jnp.arange(LANES)
    inv = ROPE_THETA ** (-(lane % half).astype(F32) / half)
    pos = jnp.where(lane[None, :] < width // 2, (t // GRID_W)[:, None], (t % GRID_W)[:, None]).astype(F32)
    ang = pos * inv[None, :]
    valid = (lane < width)[None, :]
    cos = jnp.where(valid, jnp.cos(ang), 1.0)
    first = (lane % (2 * half)) < half
    sin = jnp.where(valid, jnp.where(first[None, :], -jnp.sin(ang), jnp.sin(ang)), 0.0)
    cos = jnp.concatenate([jnp.ones((n_ctx, LANES), F32), cos], axis=0)
    sin = jnp.concatenate([jnp.zeros((n_ctx, LANES), F32), sin], axis=0)
    return cos, sin


def _rotate(x, cos, sin, half):
    lane = lax.broadcasted_iota(I32, x.shape, 1)
    first = (lane % (2 * half)) < half
    partner = jnp.where(first, pltpu.roll(x, LANES - half, 1), pltpu.roll(x, half, 1))
    return x * cos + partner * sin


def _gqa_prep_kernel(z_ref, qw_ref, kw_ref, cos_ref, sin_ref, o_ref):
    c = pl.program_id(2)
    x = z_ref[0]

    def normed(w):
        y = x * lax.rsqrt(jnp.mean(x * x, axis=-1, keepdims=True) + EPS) * w
        return _rotate(y, cos_ref[...], sin_ref[...], HEAD_DIM // 4)

    @pl.when(c < GQA_HEADS)
    def _():
        o_ref[0] = normed(qw_ref[...]).astype(BF16)

    @pl.when(jnp.logical_and(c >= GQA_HEADS, c < GQA_HEADS + GQA_KV_HEADS))
    def _():
        o_ref[0] = normed(kw_ref[...]).astype(BF16)

    @pl.when(c >= GQA_HEADS + GQA_KV_HEADS)
    def _():
        o_ref[0] = x.astype(BF16)


def _gqa_prep(z, q_norm_w, k_norm_w, cos, sin, ts=256):
    b, s, _ = z.shape
    ncol = GQA_IN_W // HEAD_DIM
    return pl.pallas_call(
        _gqa_prep_kernel,
        out_shape=jax.ShapeDtypeStruct((b, s, GQA_IN_W), BF16),
        grid=(b, s // ts, ncol),
        in_specs=[pl.BlockSpec((1, ts, HEAD_DIM), lambda bi, i, c: (bi, i, c)),
                  pl.BlockSpec((1, HEAD_DIM), lambda bi, i, c: (0, 0)),
                  pl.BlockSpec((1, HEAD_DIM), lambda bi, i, c: (0, 0)),
                  pl.BlockSpec((ts, LANES), lambda bi, i, c: (i, 0)),
                  pl.BlockSpec((ts, LANES), lambda bi, i, c: (i, 0))],
        out_specs=pl.BlockSpec((1, ts, HEAD_DIM), lambda bi, i, c: (bi, i, c)),
        compiler_params=_cparams(("parallel", "parallel", "parallel")),
        name="gqa_prep",
    )(z, q_norm_w.reshape(1, HEAD_DIM), k_norm_w.reshape(1, HEAD_DIM), cos, sin)


def _attn_kernel(q_ref, k_ref, v_ref, o_ref, *, scale, n_ctx_q_blocks, ctx_len, q_axis):
    q = q_ref[0]

    def attend(n_keys):
        k = k_ref[0, pl.ds(0, n_keys), :]
        v = v_ref[0, pl.ds(0, n_keys), :]
        s = lax.dot_general(q, k, (((1,), (1,)), ((), ())), preferred_element_type=F32) * scale
        m = jnp.max(s, axis=-1, keepdims=True)
        p = jnp.exp(s - m)
        l = jnp.sum(p, axis=-1, keepdims=True)
        o = jnp.dot(p.astype(BF16), v, preferred_element_type=F32)
        o_ref[0] = (o / l).astype(o_ref.dtype)

    n_all = k_ref.shape[1]
    if n_ctx_q_blocks == 0:
        attend(n_all)
    else:
        qi = pl.program_id(q_axis)

        @pl.when(qi < n_ctx_q_blocks)
        def _():
            attend(ctx_len)

        @pl.when(qi >= n_ctx_q_blocks)
        def _():
            attend(n_all)


def _attention(q_arr, k_arr, v_arr, *, n_heads, q_col, k_col, v_col, dk, dv, scale, q_row_block0,
               n_q_rows, n_ctx_q_blocks, ctx_len, tq=256, name="attention"):
    b, s, _ = k_arr.shape
    return pl.pallas_call(
        functools.partial(_attn_kernel, scale=scale, n_ctx_q_blocks=n_ctx_q_blocks, ctx_len=ctx_len, q_axis=2),
        out_shape=jax.ShapeDtypeStruct((b, n_q_rows, n_heads * dv), BF16),
        grid=(b, n_heads, n_q_rows // tq),
        in_specs=[pl.BlockSpec((1, tq, dk), lambda bi, h, i: (bi, i + q_row_block0, q_col(h))),
                  pl.BlockSpec((1, s, dk), lambda bi, h, i: (bi, 0, k_col(h))),
                  pl.BlockSpec((1, s, dv), lambda bi, h, i: (bi, 0, v_col(h)))],
        out_specs=pl.BlockSpec((1, tq, dv), lambda bi, h, i: (bi, i, h)),
        compiler_params=_cparams(("parallel", "parallel", "parallel")),
        name=name,
    )(q_arr, k_arr, v_arr)


def _split3(x):
    hi = x.astype(BF16)
    r1 = x - hi.astype(F32)
    mid = r1.astype(BF16)
    lo = (r1 - mid.astype(F32)).astype(BF16)
    return hi, mid, lo


def _exact_dot(a_bf16, x):
    hi, mid, lo = _split3(x)
    return (jnp.dot(a_bf16, hi, preferred_element_type=F32)
            + jnp.dot(a_bf16, mid, preferred_element_type=F32)
            + jnp.dot(a_bf16, lo, preferred_element_type=F32))


def _hgrn_kernel(q_ref, ff_ref, fb_ref, i_ref, g_ref, lbf_ref, lbb_ref, ow_ref, o_ref,
                 of_s, ob_s, cum_s, kk_s, st_s, *, n_ctx_chunks, n_chunks):
    c_len = HGRN_CHUNK
    nblk = c_len // SUBLANES
    row = lax.broadcasted_iota(I32, (c_len, c_len), 0)
    col = lax.broadcasted_iota(I32, (c_len, c_len), 1)
    tri_f = (col <= row).astype(BF16)
    tri_b = (col >= row).astype(BF16)
    sub = lax.broadcasted_iota(I32, (SUBLANES, HEAD_DIM), 0)

    def chunk(r0, f_ref, lb_ref, forward):
        di = 0 if forward else 1
        rows = pl.ds(r0, c_len)
        q = q_ref[0, rows, :]
        v = i_ref[0, rows, :]
        lb = lb_ref[0]
        sg = jax.nn.sigmoid(f_ref[0, rows, :])
        k = (1.0 - lb) * (1.0 - sg)
        g = jnp.log(lb + (1.0 - lb) * sg)
        cum = _exact_dot(tri_f if forward else tri_b, g)
        total = cum[c_len - 1:c_len, :] if forward else cum[0:1, :]
        cum_s[di] = cum
        kk_s[di] = k
        qb = [q[SUBLANES * j:SUBLANES * (j + 1)] for j in range(nblk)]
        cb = [cum[SUBLANES * j:SUBLANES * (j + 1)] for j in range(nblk)]
        ob = [jnp.zeros((SUBLANES, HEAD_DIM), F32) for _ in range(nblk)]
        for s in range(c_len):
            js = s // SUBLANES
            cs = cum_s[di, pl.ds(s, 1), :]
            ks = kk_s[di, pl.ds(s, 1), :]
            vs = i_ref[0, pl.ds(r0 + s, 1), :]
            for j in (range(js, nblk) if forward else range(0, js + 1)):
                dlt = cb[j] - cs
                if j == js:
                    keep = (sub >= s - SUBLANES * js) if forward else (sub <= s - SUBLANES * js)
                    dlt = jnp.where(keep, dlt, -jnp.inf)
                a = jnp.sum(qb[j] * ks * jnp.exp(dlt), axis=-1, keepdims=True)
                ob[j] = ob[j] + a * vs
        o = jnp.concatenate(ob, axis=0)
        st = st_s[di]
        qd = (q * jnp.exp(cum)).astype(BF16)
        o = o + lax.dot_general(qd, st.astype(BF16), (((1,), (1,)), ((), ())), preferred_element_type=F32)
        kd = (k * jnp.exp(total - cum)).astype(BF16)
        st_s[di] = st * jnp.exp(total) + lax.dot_general(v.astype(BF16), kd, (((0,), (0,)), ((), ())),
                                                        preferred_element_type=F32)
        return o

    st_s[...] = jnp.zeros_like(st_s)

    def body(j, carry):
        rf = pl.multiple_of(j * c_len, c_len)
        of_s[pl.ds(rf, c_len), :] = chunk(rf, ff_ref, lbf_ref, True)
        cbk = jnp.where(j < n_ctx_chunks, n_ctx_chunks - 1 - j, n_chunks - 1 - (j - n_ctx_chunks))
        rb = pl.multiple_of(cbk * c_len, c_len)
        ob_s[pl.ds(rb, c_len), :] = chunk(rb, fb_ref, lbb_ref, False)
        return carry

    lax.fori_loop(0, n_chunks, body, 0)

    o = of_s[...] + ob_s[...]
    y = o * lax.rsqrt(jnp.mean(o * o, axis=-1, keepdims=True) + EPS) * ow_ref[0]
    gate = g_ref[0]
    o_ref[0] = (y * (gate * jax.nn.sigmoid(gate))).astype(o_ref.dtype)


def _hgrn(z, col0, lb_fw, lb_bw, o_norm_w, ctx_len):
    b, s, _ = z.shape
    nh = HGRN_HEADS
    blk = lambda grp: pl.BlockSpec((1, s, HEAD_DIM), lambda bi, h: (bi, 0, col0 + grp * nh + h))
    vec = pl.BlockSpec((1, 1, HEAD_DIM), lambda bi, h: (h, 0, 0))
    return pl.pallas_call(
        functools.partial(_hgrn_kernel, n_ctx_chunks=ctx_len // HGRN_CHUNK, n_chunks=s // HGRN_CHUNK),
        out_shape=jax.ShapeDtypeStruct((b, s, nh * HEAD_DIM), BF16),
        grid=(b, nh),
        in_specs=[blk(0), blk(1), blk(2), blk(3), blk(4), vec, vec, vec],
        out_specs=pl.BlockSpec((1, s, HEAD_DIM), lambda bi, h: (bi, 0, h)),
        scratch_shapes=[pltpu.VMEM((s, HEAD_DIM), F32),
                        pltpu.VMEM((s, HEAD_DIM), F32),
                        pltpu.VMEM((2, HGRN_CHUNK, HEAD_DIM), F32),
                        pltpu.VMEM((2, HGRN_CHUNK, HEAD_DIM), F32),
                        pltpu.VMEM((2, HEAD_DIM, HEAD_DIM), F32)],
        compiler_params=_cparams(("parallel", "parallel")),
        name="hgrn",
    )(z, z, z, z, z, lb_fw.reshape(nh, 1, HEAD_DIM), lb_bw.reshape(nh, 1, HEAD_DIM),
      o_norm_w.reshape(nh, 1, HEAD_DIM))


NA_ROWS_PER_STEP = 4


def _na_kernel(q_ref, k_ref, v_ref, bias_ref, o_ref, kb_s, vb_s, *, scale, ctx_len, rows, win_r):
    rb = pl.program_id(2)

    @pl.when(rb == 0)
    def _():
        kb_s[...] = k_ref[0].astype(BF16)
        vb_s[...] = v_ref[0].astype(BF16)

    kc = kb_s[pl.ds(0, ctx_len), :]
    vc = vb_s[pl.ds(0, ctx_len), :]
    nt = (((1,), (1,)), ((), ()))
    for i in range(NA_ROWS_PER_STEP):
        r = rb * NA_ROWS_PER_STEP + i
        r0 = jnp.clip(r - win_r // 2, 0, rows - win_r)
        band = pl.ds(pl.multiple_of(ctx_len + r0 * GRID_W, GRID_W), win_r * GRID_W)
        q = q_ref[0, i * GRID_W:(i + 1) * GRID_W, :].astype(BF16)
        kb = kb_s[band, :]
        vb = vb_s[band, :]
        s_ctx = lax.dot_general(q, kc, nt, preferred_element_type=F32) * scale
        s_loc = lax.dot_general(q, kb, nt, preferred_element_type=F32) * scale + bias_ref[0, r - r0]
        m = jnp.maximum(jnp.max(s_ctx, axis=-1, keepdims=True), jnp.max(s_loc, axis=-1, keepdims=True))
        p_ctx = jnp.exp(s_ctx - m)
        p_loc = jnp.exp(s_loc - m)
        l = jnp.sum(p_ctx, axis=-1, keepdims=True) + jnp.sum(p_loc, axis=-1, keepdims=True)
        o = jnp.dot(p_ctx.astype(BF16), vc, preferred_element_type=F32)
        o = o + jnp.dot(p_loc.astype(BF16), vb, preferred_element_type=F32)
        o_ref[0, i * GRID_W:(i + 1) * GRID_W, :] = (o / l).astype(o_ref.dtype)


def _na_bias_table(rpb, rows, win_r):
    cols = jnp.arange(GRID_W)
    col_start = jnp.clip(cols - NA_WIN_C // 2, 0, GRID_W - NA_WIN_C)
    in_win = (cols[None, :] >= col_start[:, None]) & (cols[None, :] < col_start[:, None] + NA_WIN_C)
    dc_idx = jnp.clip(cols[None, :] - cols[:, None] + NA_WIN_C - 1, 0, 2 * NA_WIN_C - 2)
    n_off = win_r
    off = jnp.arange(n_off)
    dr_idx = jnp.arange(win_r)[None, :] - off[:, None] + NA_WIN_R_MAX - 1
    dr_ok = (dr_idx >= 0) & (dr_idx <= 2 * NA_WIN_R_MAX - 2)
    bias = rpb[:, jnp.clip(dr_idx, 0, 2 * NA_WIN_R_MAX - 2)][:, :, :, dc_idx]
    keep = in_win[None, None, None] & dr_ok[None, :, :, None, None]
    bias = jnp.where(keep, bias, NEG_BIG)
    bias = bias.transpose(0, 1, 3, 2, 4).reshape(rpb.shape[0], n_off, GRID_W, win_r * GRID_W)
    return bias.astype(F32)


def _na(z, rpb, ctx_len, n_x):
    b, s, _ = z.shape
    rows = n_x // GRID_W
    win_r = min(NA_WIN_R_MAX, rows)
    bias = _na_bias_table(rpb, rows, win_r)
    nh = NA_HEADS
    rps = NA_ROWS_PER_STEP
    assert rows % rps == 0 and ctx_len % (rps * GRID_W) == 0
    ctx_blocks = ctx_len // (rps * GRID_W)
    return pl.pallas_call(
        functools.partial(_na_kernel, scale=HEAD_DIM ** -0.5, ctx_len=ctx_len, rows=rows, win_r=win_r),
        out_shape=jax.ShapeDtypeStruct((b, n_x, nh * HEAD_DIM), BF16),
        grid=(b, nh, rows // rps),
        in_specs=[pl.BlockSpec((1, rps * GRID_W, HEAD_DIM), lambda bi, h, r: (bi, r + ctx_blocks, h)),
                  pl.BlockSpec((1, s, HEAD_DIM), lambda bi, h, r: (bi, 0, nh + h)),
                  pl.BlockSpec((1, s, HEAD_DIM), lambda bi, h, r: (bi, 0, 2 * nh + h)),
                  pl.BlockSpec((1, win_r, GRID_W, win_r * GRID_W), lambda bi, h, r: (h, 0, 0, 0))],
        out_specs=pl.BlockSpec((1, rps * GRID_W, HEAD_DIM), lambda bi, h, r: (bi, r, h)),
        scratch_shapes=[pltpu.VMEM((s, HEAD_DIM), BF16), pltpu.VMEM((s, HEAD_DIM), BF16)],
        compiler_params=_cparams(("parallel", "parallel", "arbitrary")),
        name="na",
    )(z, z, z, bias)


def _mla_prep_kernel(qup_ref, kvup_ref, kr_ref, cos_ref, sin_ref, qf_ref, kf_ref, vf_ref):
    cos = cos_ref[...]
    sin = sin_ref[...]
    qf_ref[0, :, :LANES] = qup_ref[0, :, :LANES].astype(BF16)
    qf_ref[0, :, LANES:] = _rotate(qup_ref[0, :, LANES:], cos, sin, MLA_ROPE // 4).astype(BF16)
    kf_ref[0, :, :LANES] = kvup_ref[0, :, :LANES].astype(BF16)
    kf_ref[0, :, LANES:] = _rotate(kr_ref[0], cos, sin, MLA_ROPE // 4).astype(BF16)
    vf_ref[0] = kvup_ref[0, :, LANES:].astype(BF16)


def _mla_prep(qup, kvup, z, kr_col, cos, sin, ts=256):
    b, s, _ = qup.shape
    nh = MLA_HEADS
    return pl.pallas_call(
        _mla_prep_kernel,
        out_shape=[jax.ShapeDtypeStruct((b, s, nh * MLA_QK_PAD), BF16),
                   jax.ShapeDtypeStruct((b, s, nh * MLA_QK_PAD), BF16),
                   jax.ShapeDtypeStruct((b, s, nh * MLA_V), BF16)],
        grid=(b, s // ts, nh),
        in_specs=[pl.BlockSpec((1, ts, MLA_QK_PAD), lambda bi, i, h: (bi, i, h)),
                  pl.BlockSpec((1, ts, MLA_NOPE + MLA_V), lambda bi, i, h: (bi, i, h)),
                  pl.BlockSpec((1, ts, LANES), lambda bi, i, h: (bi, i, kr_col)),
                  pl.BlockSpec((ts, LANES), lambda bi, i, h: (i, 0)),
                  pl.BlockSpec((ts, LANES), lambda bi, i, h: (i, 0))],
        out_specs=[pl.BlockSpec((1, ts, MLA_QK_PAD), lambda bi, i, h: (bi, i, h)),
                   pl.BlockSpec((1, ts, MLA_QK_PAD), lambda bi, i, h: (bi, i, h)),
                   pl.BlockSpec((1, ts, MLA_V), lambda bi, i, h: (bi, i, h))],
        compiler_params=_cparams(("parallel", "parallel", "parallel")),
        name="mla_prep",
    )(qup, kvup, z, cos, sin)


def _topk_rows(x, k, payload=None):
    n, width = x.shape
    row = lax.broadcasted_iota(I32, (n, width), 0).astype(F32)
    out_row = lax.broadcasted_iota(I32, (k, width), 0)
    vals = jnp.zeros((k, width), F32)
    poss = jnp.zeros((k, width), F32)
    pays = jnp.zeros((k, width), F32)
    for r in range(k):
        m = jnp.max(x, axis=0, keepdims=True)
        pos = jnp.min(jnp.where(x == m, row, float(n)), axis=0, keepdims=True)
        hit = row == pos
        vals = jnp.where(out_row == r, m, vals)
        poss = jnp.where(out_row == r, pos, poss)
        if payload is not None:
            pay = jnp.max(jnp.where(hit, payload, -1.0), axis=0, keepdims=True)
            pays = jnp.where(out_row == r, pay, pays)
        x = jnp.where(hit, -jnp.inf, x)
    return vals, poss, pays


def _peer_route_kernel(q_ref, keys_ref, gate_ref, eidx_ref):
    kk = PEER_TOPK
    nt = (((1,), (1,)), ((), ()))
    for h in range(PEER_HEADS):
        sv, si = [], []
        for p in range(2):
            j = 2 * h + p
            qj = q_ref[:, j * PEER_HALF:(j + 1) * PEER_HALF].astype(BF16)
            s = lax.dot_general(keys_ref[j], qj, nt, preferred_element_type=F32)
            v, pos, _ = _topk_rows(s, kk)
            sv.append(v)
            si.append(pos)
        cand = jnp.concatenate([sv[0][a:a + 1, :] + sv[1] for a in range(kk)], axis=0)
        cidx = jnp.concatenate([si[0][a:a + 1, :] * float(PEER_N_KEYS) + si[1] for a in range(kk)], axis=0)
        top_s, _, eidx = _topk_rows(cand, kk, payload=cidx)
        e = jnp.exp(top_s - top_s[0:1, :])
        gate_ref[h] = e / jnp.sum(e, axis=0, keepdims=True)
        eidx_ref[h] = eidx.astype(I32)


def _peer_route(q, keys, tm=256):
    t = q.shape[0]
    nh, kk = PEER_HEADS, PEER_TOPK
    gate, eidx = pl.pallas_call(
        _peer_route_kernel,
        out_shape=[jax.ShapeDtypeStruct((nh, kk, t), F32), jax.ShapeDtypeStruct((nh, kk, t), I32)],
        grid=(t // tm,),
        in_specs=[pl.BlockSpec((tm, q.shape[1]), lambda i: (i, 0)),
                  pl.BlockSpec(keys.shape, lambda i: (0, 0, 0))],
        out_specs=[pl.BlockSpec((nh, kk, tm), lambda i: (0, 0, i)),
                   pl.BlockSpec((nh, kk, tm), lambda i: (0, 0, i))],
        compiler_params=_cparams(("parallel",)),
        name="peer_route",
    )(q, keys)
    return gate.reshape(nh * kk, t).T, eidx.reshape(nh * kk, t).T


def _gelu_tanh(x):
    return 0.5 * x * (1.0 + jnp.tanh(math.sqrt(2.0 / math.pi) * (x + 0.044715 * x * x * x)))


def _pack_tables(u, v):
    ub = lax.bitcast_convert_type(u.astype(BF16), jnp.uint16).astype(jnp.uint32)
    vb = lax.bitcast_convert_type(v.astype(BF16), jnp.uint16).astype(jnp.uint32)
    return (ub | (vb << 16))[:, None, :]


SUBLANES = 8


def _peer_expert_kernel(idx_cur, idx_nxt, f_ref, gate_ref, h_ref, mod_ref, fw_ref, uv_hbm, o_ref,
                        buf, sem, *, tb, n_sel, final_norm):
    i = pl.program_id(0)
    n = pl.num_programs(0)
    slot = i % 2
    nxt = 1 - slot
    groups = n_sel // SUBLANES
    half = groups // 2

    def issue(idx_ref, s, t, g_lo, g_hi):
        for g in range(g_lo, g_hi):
            for k in range(SUBLANES):
                row = idx_ref[t * n_sel + (g * SUBLANES + k)]
                pltpu.make_async_copy(uv_hbm.at[row], buf.at[s, t * groups + g, pl.ds(k, 1), :],
                                      sem.at[s]).start(priority=k % 2)

    def wait(s):
        pltpu.make_async_copy(buf.at[s], buf.at[s], sem.at[s]).wait()

    @pl.when(i == 0)
    def _():
        def first(t, carry):
            issue(idx_cur, 0, t, 0, groups)
            return carry
        lax.fori_loop(0, tb, first, 0)

    wait(slot)

    x = f_ref[...]
    tok_row = lax.broadcasted_iota(I32, (tb, n_sel), 0)
    nt = (((1,), (1,)), ((), ()))
    d = buf.shape[-1]

    def words(t):
        return buf[slot, pl.ds(pl.multiple_of(t * groups, groups), groups)].reshape(n_sel, d)

    def score(t, hh):
        u_t = lax.bitcast_convert_type(words(t) << 16, F32).astype(BF16)
        s = lax.dot_general(x, u_t, nt, preferred_element_type=F32)
        issue(idx_nxt, nxt, t, 0, half)
        return jnp.where(tok_row == t, s, hh)

    hh = lax.fori_loop(0, tb, score, jnp.zeros((tb, n_sel), F32), unroll=2)
    act = gate_ref[...] * _gelu_tanh(hh)

    def mix(t, acc):
        v_t = lax.bitcast_convert_type(words(t) & jnp.uint32(0xFFFF0000), F32).astype(BF16)
        wm = jnp.where(tok_row == t, act, 0.0).astype(BF16)
        y = jnp.dot(wm, v_t, preferred_element_type=F32)
        issue(idx_nxt, nxt, t, half, groups)
        return acc + y

    y = lax.fori_loop(0, tb, mix, jnp.zeros(o_ref.shape, F32), unroll=2)
    out = h_ref[...] + mod_ref[0, pl.ds(5, 1), :] * y
    if final_norm:
        out = out * lax.rsqrt(jnp.mean(out * out, axis=-1, keepdims=True) + EPS) * fw_ref[...]
    o_ref[...] = out

    @pl.when(i == n - 1)
    def _():
        wait(nxt)


def _peer_experts(f, gate, eidx, h, mod, uv, final_w, *, rows_per_batch, n_ctx_rows, final_norm, tb=16):
    t, d = f.shape
    n_sel = gate.shape[1]
    nb = mod.shape[0] - 1
    bpb = rows_per_batch // tb
    ctx_blocks = n_ctx_rows // tb
    n_blocks = t // tb
    eflat = eidx.reshape(t * n_sel)

    def mod_map(i):
        return (jnp.where(i % bpb < ctx_blocks, nb, i // bpb), 0, 0)

    smem = pltpu.SMEM
    return pl.pallas_call(
        functools.partial(_peer_expert_kernel, tb=tb, n_sel=n_sel, final_norm=final_norm),
        out_shape=jax.ShapeDtypeStruct((t, d), F32),
        grid=(n_blocks,),
        in_specs=[pl.BlockSpec((tb * n_sel,), lambda i: (i,), memory_space=smem),
                  pl.BlockSpec((tb * n_sel,), lambda i: (jnp.minimum(i + 1, n_blocks - 1),), memory_space=smem),
                  pl.BlockSpec((tb, d), lambda i: (i, 0)),
                  pl.BlockSpec((tb, n_sel), lambda i: (i, 0)),
                  pl.BlockSpec((tb, d), lambda i: (i, 0)),
                  pl.BlockSpec((1, 6, d), mod_map),
                  pl.BlockSpec((1, d), lambda i: (0, 0)),
                  pl.BlockSpec(memory_space=pl.ANY)],
        out_specs=pl.BlockSpec((tb, d), lambda i: (i, 0)),
        scratch_shapes=[pltpu.VMEM((2, tb * n_sel // SUBLANES, SUBLANES, d), jnp.uint32),
                        pltpu.SemaphoreType.DMA((2,))],
        compiler_params=_cparams(("arbitrary",)),
        name="peer_experts",
    )(eflat, eflat, f, gate, h, mod, final_w.reshape(1, d).astype(F32), uv)


def _peer(h, mod, norm_w, wq, keys, u, v, final_w, *, n_ctx_blocks, n_ctx_rows, final_norm, tm=256):
    b, r, d = h.shape
    q, f = _norm_matmul(h, norm_w, wq.astype(BF16), mod=mod, shift_row=3, n_ctx_blocks=n_ctx_blocks,
                        tm=tm, emit_a=True, name="peer_query")
    keys2 = keys.reshape(PEER_HEADS * 2, PEER_N_KEYS, PEER_HALF).astype(BF16)
    gate, eidx = _peer_route(q.reshape(b * r, -1), keys2, tm=tm)
    out = _peer_experts(f.reshape(b * r, d), gate, eidx, h.reshape(b * r, d), mod, _pack_tables(u, v), final_w,
                        rows_per_batch=r, n_ctx_rows=n_ctx_rows, final_norm=final_norm)
    return out.reshape(b, r, d)


def kernel(x, c, ctx, c_ctx, w_mod, b_mod, norm1_w, norm2_w, ev_w_in, ev_w_out, gqa_q_norm, gqa_k_norm, hgrn_lb, hgrn_o_norm, od_w_in, od_w_out, na_rpb, mla_q_norm, mla_w_q_up, mla_kv_norm, mla_w_kv_up, peer_wq, peer_keys, peer_u, peer_v, final_norm_w):
    b, n_x, d = x.shape
    n_ctx = ctx.shape[1]
    depth = w_mod.shape[0]
    tm = 256
    assert n_ctx % tm == 0 and n_x % tm == 0 and depth == 2
    ctx_blocks = n_ctx // tm

    h = jnp.concatenate([ctx, x], axis=1)
    c_all = jnp.concatenate([c, c_ctx[None, :]], axis=0)
    m_pad = -(-c_all.shape[0] // 8) * 8
    c_pad = jnp.pad(c_all, ((0, m_pad - c_all.shape[0]), (0, 0)))
    lb_table = jnp.cumsum(jax.nn.softmax(hgrn_lb.astype(F32), axis=1), axis=1)

    mod0 = _modulation(c_pad, w_mod[0], b_mod[0])[:b + 1].reshape(b + 1, 6, d)
    z = _norm_matmul(h, norm1_w[0], ev_w_in[0].astype(BF16), mod=mod0, shift_row=0, n_ctx_blocks=ctx_blocks,
                     tm=tm, name="in_proj0")
    cos_g, sin_g = _rope_tables(n_ctx, n_x, HEAD_DIM, HEAD_DIM // 4)
    qkv = _gqa_prep(z, gqa_q_norm[0], gqa_k_norm[0], cos_g, sin_g)
    ya = _attention(qkv, qkv, qkv, n_heads=GQA_HEADS, q_col=lambda hh: hh,
                    k_col=lambda hh: GQA_HEADS + hh // GQA_GROUP,
                    v_col=lambda hh: GQA_HEADS + GQA_KV_HEADS + hh // GQA_GROUP,
                    dk=HEAD_DIM, dv=HEAD_DIM, scale=HEAD_DIM ** -0.5, q_row_block0=0, n_q_rows=n_ctx + n_x,
                    n_ctx_q_blocks=ctx_blocks, ctx_len=n_ctx, tq=tm, name="gqa_attention")
    yb = _hgrn(z, GQA_IN_W // HEAD_DIM, lb_table[0, 0], lb_table[1, 0], hgrn_o_norm[0], n_ctx)
    h = _out_proj(ya, yb, ev_w_out[0], h, mod0, n_ctx_blocks=ctx_blocks, tm=tm)
    h = _peer(h, mod0, norm2_w[0], peer_wq[0], peer_keys[0], peer_u[0], peer_v[0], final_norm_w,
              n_ctx_blocks=ctx_blocks, n_ctx_rows=n_ctx, final_norm=False, tm=tm)

    mod1 = _modulation(c_pad, w_mod[1], b_mod[1])[:b + 1].reshape(b + 1, 6, d)
    odd_w = od_w_in[0]
    kr_pad = LANES - MLA_ROPE
    odd_w = jnp.pad(odd_w, ((0, 0), (0, kr_pad))).astype(BF16)
    z1 = _norm_matmul(h, norm1_w[1], odd_w, mod=mod1, shift_row=0, n_ctx_blocks=ctx_blocks, tm=tm,
                      tn=odd_w.shape[1] // 3, name="in_proj1")
    ya1 = _na(z1, na_rpb[0], n_ctx, n_x)

    qw = mla_w_q_up[0].reshape(MLA_Q_RANK, MLA_HEADS, MLA_NOPE + MLA_ROPE)
    qw = jnp.pad(qw, ((0, 0), (0, 0), (0, MLA_QK_PAD - MLA_NOPE - MLA_ROPE)))
    qw = qw.reshape(MLA_Q_RANK, MLA_HEADS * MLA_QK_PAD).astype(BF16)
    cq_col = NA_IN_W // MLA_Q_RANK
    qup = _norm_matmul(z1, mla_q_norm[0], qw, col_block=cq_col, tm=tm, name="mla_q_up")
    kvup = _norm_matmul(z1, mla_kv_norm[0], mla_w_kv_up[0].astype(BF16), col_block=cq_col + 1, tm=tm,
                        name="mla_kv_up")
    cos_m, sin_m = _rope_tables(n_ctx, n_x, MLA_ROPE, MLA_ROPE // 4)
    qf, kf, vf = _mla_prep(qup, kvup, z1, (NA_IN_W + MLA_Q_RANK + MLA_KV_RANK) // LANES, cos_m, sin_m)
    yb1 = _attention(qf, kf, vf, n_heads=MLA_HEADS, q_col=lambda hh: hh, k_col=lambda hh: hh,
                     v_col=lambda hh: hh, dk=MLA_QK_PAD, dv=MLA_V, scale=(MLA_NOPE + MLA_ROPE) ** -0.5,
                     q_row_block0=ctx_blocks, n_q_rows=n_x, n_ctx_q_blocks=0, ctx_len=n_ctx, tq=tm,
                     name="mla_attention")
    hx = _out_proj(ya1, yb1, od_w_out[0], h, mod1, n_ctx_blocks=0, h_row_block0=ctx_blocks, tm=tm)
    return _peer(hx, mod1, norm2_w[1], peer_wq[1], peer_keys[1], peer_u[1], peer_v[1], final_norm_w,
                 n_ctx_blocks=0, n_ctx_rows=0, final_norm=True, tm=tm)
```

```python
import functools
import math

import jax
import jax.numpy as jnp
from jax import lax
from jax.experimental import pallas as pl
from jax.experimental.pallas import tpu as pltpu

F32 = jnp.float32
BF16 = jnp.bfloat16
I32 = jnp.int32

EPS = 1e-6
ROPE_THETA = 10000.0
GRID_W = 64
HEAD_DIM = 128
LANES = 128
SUBLANES = 8

GQA_HEADS = 8
GQA_KV_HEADS = 2
GQA_GROUP = GQA_HEADS // GQA_KV_HEADS
GQA_IN_W = (GQA_HEADS + 2 * GQA_KV_HEADS) * HEAD_DIM
HGRN_HEADS = 8
HGRN_CHUNK = 32
HGRN_HEADS_PER_STEP = 2
NA_HEADS = 8
NA_WIN_R_MAX = 8
NA_WIN_C = 16
NA_IN_W = 3 * NA_HEADS * HEAD_DIM
NA_ROWS_PER_STEP = 4
MLA_HEADS = 8
MLA_Q_RANK = 512
MLA_KV_RANK = 512
MLA_NOPE = 128
MLA_ROPE = 64
MLA_V = 128
MLA_QK_PAD = 256
PEER_HEADS = 8
PEER_N_KEYS = 128
PEER_TOPK = 16
PEER_HALF = 128

VMEM_LIMIT = 56 * 1024 * 1024
NEG_BIG = -1e30


def _cparams(sem):
    return pltpu.CompilerParams(dimension_semantics=sem, vmem_limit_bytes=VMEM_LIMIT)


def _mod_kernel(c_ref, w_ref, b_ref, o_ref):
    c = c_ref[...]
    a = (c * jax.nn.sigmoid(c)).astype(BF16)
    o_ref[...] = jnp.dot(a, w_ref[...].astype(BF16), preferred_element_type=F32) + b_ref[...]


def _modulation(c_all, w, b, tn=1024):
    m, d = c_all.shape
    n = w.shape[1]
    return pl.pallas_call(
        _mod_kernel,
        out_shape=jax.ShapeDtypeStruct((m, n), F32),
        grid=(n // tn,),
        in_specs=[pl.BlockSpec((m, d), lambda j: (0, 0)),
                  pl.BlockSpec((d, tn), lambda j: (0, j)),
                  pl.BlockSpec((1, tn), lambda j: (0, j))],
        out_specs=pl.BlockSpec((m, tn), lambda j: (0, j)),
        compiler_params=_cparams(("parallel",)),
        name="modulation",
    )(c_all, w, b.reshape(1, n))


def _nmm_kernel(*refs, modulate, shift_row, emit_a):
    if modulate:
        h_ref, nw_ref, mod_ref, w_ref = refs[:4]
        rest = refs[4:]
    else:
        h_ref, nw_ref, w_ref = refs[:3]
        mod_ref = None
        rest = refs[3:]
    if emit_a:
        o_ref, a_ref, a_s = rest
    else:
        o_ref, a_s = rest
        a_ref = None

    @pl.when(pl.program_id(2) == 0)
    def _():
        x = h_ref[0].astype(F32)
        y = x * lax.rsqrt(jnp.mean(x * x, axis=-1, keepdims=True) + EPS) * nw_ref[...]
        if modulate:
            shift = mod_ref[0, pl.ds(shift_row, 1), :]
            scale = mod_ref[0, pl.ds(shift_row + 1, 1), :]
            y = y * (1.0 + scale) + shift
        a_s[...] = y.astype(BF16)
        if emit_a:
            a_ref[0] = y.astype(BF16)

    o_ref[0] = jnp.dot(a_s[...], w_ref[...], preferred_element_type=F32).astype(o_ref.dtype)


def _norm_matmul(h, norm_w, w, *, mod=None, shift_row=0, n_ctx_blocks=0, col_block=0, row_block0=0,
                 n_rows=None, tm=256, tn=512, emit_a=False, out_dtype=F32, name="norm_matmul"):
    b, s, _ = h.shape
    k, n = w.shape
    n_rows = s if n_rows is None else n_rows
    nb = b
    modulate = mod is not None
    grid = (b, n_rows // tm, n // tn)
    in_specs = [pl.BlockSpec((1, tm, k), lambda bi, i, j: (bi, i + row_block0, col_block)),
                pl.BlockSpec((1, k), lambda bi, i, j: (0, 0))]
    args = [h, norm_w.reshape(1, k).astype(F32)]
    if modulate:
        in_specs.append(pl.BlockSpec((1, 6, k), lambda bi, i, j: (jnp.where(i < n_ctx_blocks, nb, bi), 0, 0)))
        args.append(mod)
    in_specs.append(pl.BlockSpec((k, tn), lambda bi, i, j: (0, j)))
    args.append(w)
    out_shape = [jax.ShapeDtypeStruct((b, n_rows, n), out_dtype)]
    out_specs = [pl.BlockSpec((1, tm, tn), lambda bi, i, j: (bi, i, j))]
    if emit_a:
        out_shape.append(jax.ShapeDtypeStruct((b, n_rows, k), BF16))
        out_specs.append(pl.BlockSpec((1, tm, k), lambda bi, i, j: (bi, i, 0)))
    res = pl.pallas_call(
        functools.partial(_nmm_kernel, modulate=modulate, shift_row=shift_row, emit_a=emit_a),
        out_shape=out_shape, grid=grid, in_specs=in_specs, out_specs=out_specs,
        scratch_shapes=[pltpu.VMEM((tm, k), BF16)],
        compiler_params=_cparams(("parallel", "parallel", "arbitrary")),
        name=name,
    )(*args)
    return res if emit_a else res[0]


def _outproj_kernel(ya_ref, yb_ref, wa_ref, wb_ref, h_ref, mod_ref, o_ref, *, gate_row):
    acc = jnp.dot(ya_ref[0], wa_ref[...], preferred_element_type=F32)
    acc = acc + jnp.dot(yb_ref[0], wb_ref[...], preferred_element_type=F32)
    gate = mod_ref[0, pl.ds(gate_row, 1), :]
    o_ref[0] = h_ref[0] + gate * acc


def _out_proj(ya, yb, w_out, h, mod, *, n_ctx_blocks, h_row_block0=0, tm=256, tn=1024):
    b, n_rows, ka = ya.shape
    kb = yb.shape[2]
    d = w_out.shape[1]
    nb = b
    wa = w_out[:ka].astype(BF16)
    wb = w_out[ka:].astype(BF16)
    return pl.pallas_call(
        functools.partial(_outproj_kernel, gate_row=2),
        out_shape=jax.ShapeDtypeStruct((b, n_rows, d), F32),
        grid=(b, n_rows // tm, d // tn),
        in_specs=[pl.BlockSpec((1, tm, ka), lambda bi, i, j: (bi, i, 0)),
                  pl.BlockSpec((1, tm, kb), lambda bi, i, j: (bi, i, 0)),
                  pl.BlockSpec((ka, tn), lambda bi, i, j: (0, j)),
                  pl.BlockSpec((kb, tn), lambda bi, i, j: (0, j)),
                  pl.BlockSpec((1, tm, tn), lambda bi, i, j: (bi, i + h_row_block0, j)),
                  pl.BlockSpec((1, 6, tn), lambda bi, i, j: (jnp.where(i < n_ctx_blocks, nb, bi), 0, j))],
        out_specs=pl.BlockSpec((1, tm, tn), lambda bi, i, j: (bi, i, j)),
        compiler_params=_cparams(("parallel", "parallel", "parallel")),
        name="out_proj",
    )(ya, yb, wa, wb, h, mod)


def _rope_tables(n_ctx, n_x, width, half):
    t = jnp.arange(n_x)
    lane = jnp.arange(LANES)
    inv = ROPE_THETA ** (-(lane % half).astype(F32) / half)
    pos = jnp.where(lane[None, :] < width // 2, (t // GRID_W)[:, None], (t % GRID_W)[:, None]).astype(F32)
    ang = pos * inv[None, :]
    valid = (lane < width)[None, :]
    cos = jnp.where(valid, jnp.cos(ang), 1.0)
    first = (lane % (2 * half)) < half
    sin = jnp.where(valid, jnp.where(first[None, :], -jnp.sin(ang), jnp.sin(ang)), 0.0)
    cos = jnp.concatenate([jnp.ones((n_ctx, LANES), F32), cos], axis=0)
    sin = jnp.concatenate([jnp.zeros((n_ctx, LANES), F32), sin], axis=0)
    return cos, sin


def _rotate(x, cos, sin, half):
    lane = lax.broadcasted_iota(I32, x.shape, 1)
    first = (lane % (2 * half)) < half
    partner = jnp.where(first, pltpu.roll(x, LANES - half, 1), pltpu.roll(x, half, 1))
    return x * cos + partner * sin


def _gqa_prep_kernel(z_ref, qw_ref, kw_ref, cos_ref, sin_ref, o_ref):
    cos = cos_ref[...]
    sin = sin_ref[...]
    for c in range(GQA_IN_W // HEAD_DIM):
        cols = slice(c * HEAD_DIM, (c + 1) * HEAD_DIM)
        x = z_ref[0, :, cols]
        if c < GQA_HEADS + GQA_KV_HEADS:
            w = qw_ref[...] if c < GQA_HEADS else kw_ref[...]
            y = x * lax.rsqrt(jnp.mean(x * x, axis=-1, keepdims=True) + EPS) * w
            x = _rotate(y, cos, sin, HEAD_DIM // 4)
        o_ref[0, :, cols] = x.astype(BF16)


def _gqa_prep(z, q_norm_w, k_norm_w, cos, sin, ts=256):
    b, s, _ = z.shape
    return pl.pallas_call(
        _gqa_prep_kernel,
        out_shape=jax.ShapeDtypeStruct((b, s, GQA_IN_W), BF16),
        grid=(b, s // ts),
        in_specs=[pl.BlockSpec((1, ts, GQA_IN_W), lambda bi, i: (bi, i, 0)),
                  pl.BlockSpec((1, HEAD_DIM), lambda bi, i: (0, 0)),
                  pl.BlockSpec((1, HEAD_DIM), lambda bi, i: (0, 0)),
                  pl.BlockSpec((ts, LANES), lambda bi, i: (i, 0)),
                  pl.BlockSpec((ts, LANES), lambda bi, i: (i, 0))],
        out_specs=pl.BlockSpec((1, ts, GQA_IN_W), lambda bi, i: (bi, i, 0)),
        compiler_params=_cparams(("parallel", "parallel")),
        name="gqa_prep",
    )(z, q_norm_w.reshape(1, HEAD_DIM), k_norm_w.reshape(1, HEAD_DIM), cos, sin)


def _attn_kernel(q_ref, k_ref, v_ref, o_ref, *, scale, n_ctx_q_blocks, ctx_len, q_axis):
    q = q_ref[0]

    def attend(n_keys):
        k = k_ref[0, pl.ds(0, n_keys), :]
        v = v_ref[0, pl.ds(0, n_keys), :]
        s = lax.dot_general(q, k, (((1,), (1,)), ((), ())), preferred_element_type=F32) * scale
        m = jnp.max(s, axis=-1, keepdims=True)
        p = jnp.exp(s - m)
        l = jnp.sum(p, axis=-1, keepdims=True)
        o = jnp.dot(p.astype(BF16), v, preferred_element_type=F32)
        o_ref[0] = (o / l).astype(o_ref.dtype)

    n_all = k_ref.shape[1]
    if n_ctx_q_blocks == 0:
        attend(n_all)
    else:
        qi = pl.program_id(q_axis)

        @pl.when(qi < n_ctx_q_blocks)
        def _():
            attend(ctx_len)

        @pl.when(qi >= n_ctx_q_blocks)
        def _():
            attend(n_all)


def _attention(q_arr, k_arr, v_arr, *, n_heads, q_col, k_col, v_col, dk, dv, scale, q_row_block0,
               n_q_rows, n_ctx_q_blocks, ctx_len, tq=256, name="attention"):
    b, s, _ = k_arr.shape
    return pl.pallas_call(
        functools.partial(_attn_kernel, scale=scale, n_ctx_q_blocks=n_ctx_q_blocks, ctx_len=ctx_len, q_axis=2),
        out_shape=jax.ShapeDtypeStruct((b, n_q_rows, n_heads * dv), BF16),
        grid=(b, n_heads, n_q_rows // tq),
        in_specs=[pl.BlockSpec((1, tq, dk), lambda bi, h, i: (bi, i + q_row_block0, q_col(h))),
                  pl.BlockSpec((1, s, dk), lambda bi, h, i: (bi, 0, k_col(h))),
                  pl.BlockSpec((1, s, dv), lambda bi, h, i: (bi, 0, v_col(h)))],
        out_specs=pl.BlockSpec((1, tq, dv), lambda bi, h, i: (bi, i, h)),
        compiler_params=_cparams(("parallel", "parallel", "parallel")),
        name=name,
    )(q_arr, k_arr, v_arr)


def _split3(x):
    hi = x.astype(BF16)
    r1 = x - hi.astype(F32)
    mid = r1.astype(BF16)
    lo = (r1 - mid.astype(F32)).astype(BF16)
    return hi, mid, lo


def _exact_dot(a_bf16, x):
    hi, mid, lo = _split3(x)
    return (jnp.dot(a_bf16, hi, preferred_element_type=F32)
            + jnp.dot(a_bf16, mid, preferred_element_type=F32)
            + jnp.dot(a_bf16, lo, preferred_element_type=F32))


def _hgrn_kernel(q_ref, ff_ref, fb_ref, i_ref, g_ref, lbf_ref, lbb_ref, ow_ref, o_ref,
                 of_s, ob_s, cum_s, kk_s, vv_s, st_s, *, n_ctx_chunks, n_chunks):
    c_len = HGRN_CHUNK
    nblk = c_len // SUBLANES
    hps = HGRN_HEADS_PER_STEP
    row = lax.broadcasted_iota(I32, (c_len, c_len), 0)
    col = lax.broadcasted_iota(I32, (c_len, c_len), 1)
    tri_f = (col <= row).astype(BF16)
    tri_b = (col >= row).astype(BF16)
    sub = lax.broadcasted_iota(I32, (SUBLANES, HEAD_DIM), 0)

    def chunk(r0, f_ref, lb_ref, forward, hh):
        ci = (0 if forward else hps) + hh
        lanes = slice(hh * HEAD_DIM, (hh + 1) * HEAD_DIM)
        rows = pl.ds(r0, c_len)
        q = q_ref[0, rows, lanes]
        v = i_ref[0, rows, lanes]
        lb = lb_ref[hh]
        sg = jax.nn.sigmoid(f_ref[0, rows, lanes])
        k = (1.0 - lb) * (1.0 - sg)
        g = jnp.log(lb + (1.0 - lb) * sg)
        cum = _exact_dot(tri_f if forward else tri_b, g)
        total = cum[c_len - 1:c_len, :] if forward else cum[0:1, :]
        cum_s[ci] = cum
        kk_s[ci] = k
        vv_s[ci] = v
        qb = [q[SUBLANES * j:SUBLANES * (j + 1)] for j in range(nblk)]
        cb = [cum[SUBLANES * j:SUBLANES * (j + 1)] for j in range(nblk)]
        ob = [jnp.zeros((SUBLANES, HEAD_DIM), F32) for _ in range(nblk)]
        for s in range(c_len):
            js = s // SUBLANES
            cs = cum_s[ci, pl.ds(s, 1), :]
            ks = kk_s[ci, pl.ds(s, 1), :]
            vs = vv_s[ci, pl.ds(s, 1), :]
            for j in (range(js, nblk) if forward else range(0, js + 1)):
                dlt = cb[j] - cs
                if j == js:
                    keep = (sub >= s - SUBLANES * js) if forward else (sub <= s - SUBLANES * js)
                    dlt = jnp.where(keep, dlt, -jnp.inf)
                a = jnp.sum(qb[j] * ks * jnp.exp(dlt), axis=-1, keepdims=True)
                ob[j] = ob[j] + a * vs
        o = jnp.concatenate(ob, axis=0)
        st = st_s[ci]
        qd = (q * jnp.exp(cum)).astype(BF16)
        o = o + lax.dot_general(qd, st.astype(BF16), (((1,), (1,)), ((), ())), preferred_element_type=F32)
        kd = (k * jnp.exp(total - cum)).astype(BF16)
        st_s[ci] = st * jnp.exp(total) + lax.dot_general(v.astype(BF16), kd, (((0,), (0,)), ((), ())),
                                                        preferred_element_type=F32)
        return o

    st_s[...] = jnp.zeros_like(st_s)

    def body(j, carry):
        rf = pl.multiple_of(j * c_len, c_len)
        cbk = jnp.where(j < n_ctx_chunks, n_ctx_chunks - 1 - j, n_chunks - 1 - (j - n_ctx_chunks))
        rb = pl.multiple_of(cbk * c_len, c_len)
        for hh in range(hps):
            lanes = slice(hh * HEAD_DIM, (hh + 1) * HEAD_DIM)
            of_s[pl.ds(rf, c_len), lanes] = chunk(rf, ff_ref, lbf_ref, True, hh)
            ob_s[pl.ds(rb, c_len), lanes] = chunk(rb, fb_ref, lbb_ref, False, hh)
        return carry

    lax.fori_loop(0, n_chunks, body, 0)

    for hh in range(hps):
        lanes = slice(hh * HEAD_DIM, (hh + 1) * HEAD_DIM)
        o = of_s[:, lanes] + ob_s[:, lanes]
        y = o * lax.rsqrt(jnp.mean(o * o, axis=-1, keepdims=True) + EPS) * ow_ref[hh]
        gate = g_ref[0, :, lanes]
        o_ref[0, :, lanes] = (y * (gate * jax.nn.sigmoid(gate))).astype(o_ref.dtype)


def _hgrn(z, col0, lb_fw, lb_bw, o_norm_w, ctx_len):
    b, s, _ = z.shape
    nh = HGRN_HEADS
    hps = HGRN_HEADS_PER_STEP
    wide = hps * HEAD_DIM
    assert nh % hps == 0 and col0 % hps == 0
    blk = lambda grp: pl.BlockSpec((1, s, wide), lambda bi, hp: (bi, 0, (col0 + grp * nh) // hps + hp))
    vec = pl.BlockSpec((hps, 1, HEAD_DIM), lambda bi, hp: (hp, 0, 0))
    return pl.pallas_call(
        functools.partial(_hgrn_kernel, n_ctx_chunks=ctx_len // HGRN_CHUNK, n_chunks=s // HGRN_CHUNK),
        out_shape=jax.ShapeDtypeStruct((b, s, nh * HEAD_DIM), BF16),
        grid=(b, nh // hps),
        in_specs=[blk(0), blk(1), blk(2), blk(3), blk(4), vec, vec, vec],
        out_specs=pl.BlockSpec((1, s, wide), lambda bi, hp: (bi, 0, hp)),
        scratch_shapes=[pltpu.VMEM((s, wide), F32),
                        pltpu.VMEM((s, wide), F32),
                        pltpu.VMEM((2 * hps, HGRN_CHUNK, HEAD_DIM), F32),
                        pltpu.VMEM((2 * hps, HGRN_CHUNK, HEAD_DIM), F32),
                        pltpu.VMEM((2 * hps, HGRN_CHUNK, HEAD_DIM), F32),
                        pltpu.VMEM((2 * hps, HEAD_DIM, HEAD_DIM), F32)],
        compiler_params=_cparams(("parallel", "parallel")),
        name="hgrn",
    )(z, z, z, z, z, lb_fw.reshape(nh, 1, HEAD_DIM), lb_bw.reshape(nh, 1, HEAD_DIM),
      o_norm_w.reshape(nh, 1, HEAD_DIM))


def _na_kernel(q_ref, k_ref, v_ref, bias_ref, o_ref, kb_s, vb_s, *, scale, ctx_len, rows, win_r):
    rb = pl.program_id(2)

    @pl.when(rb == 0)
    def _():
        kb_s[...] = k_ref[0].astype(BF16)
        vb_s[...] = v_ref[0].astype(BF16)

    kc = kb_s[pl.ds(0, ctx_len), :]
    vc = vb_s[pl.ds(0, ctx_len), :]
    nt = (((1,), (1,)), ((), ()))
    for i in range(NA_ROWS_PER_STEP):
        r = rb * NA_ROWS_PER_STEP + i
        r0 = jnp.clip(r - win_r // 2, 0, rows - win_r)
        band = pl.ds(pl.multiple_of(ctx_len + r0 * GRID_W, GRID_W), win_r * GRID_W)
        q = q_ref[0, i * GRID_W:(i + 1) * GRID_W, :].astype(BF16)
        kb = kb_s[band, :]
        vb = vb_s[band, :]
        s_ctx = lax.dot_general(q, kc, nt, preferred_element_type=F32) * scale
        s_loc = lax.dot_general(q, kb, nt, preferred_element_type=F32) * scale + bias_ref[0, r - r0]
        m = jnp.maximum(jnp.max(s_ctx, axis=-1, keepdims=True), jnp.max(s_loc, axis=-1, keepdims=True))
        p_ctx = jnp.exp(s_ctx - m)
        p_loc = jnp.exp(s_loc - m)
        l = jnp.sum(p_ctx, axis=-1, keepdims=True) + jnp.sum(p_loc, axis=-1, keepdims=True)
        o = jnp.dot(p_ctx.astype(BF16), vc, preferred_element_type=F32)
        o = o + jnp.dot(p_loc.astype(BF16), vb, preferred_element_type=F32)
        o_ref[0, i * GRID_W:(i + 1) * GRID_W, :] = (o / l).astype(o_ref.dtype)


def _na_bias_table(rpb, rows, win_r):
    cols = jnp.arange(GRID_W)
    col_start = jnp.clip(cols - NA_WIN_C // 2, 0, GRID_W - NA_WIN_C)
    in_win = (cols[None, :] >= col_start[:, None]) & (cols[None, :] < col_start[:, None] + NA_WIN_C)
    dc_idx = jnp.clip(cols[None, :] - cols[:, None] + NA_WIN_C - 1, 0, 2 * NA_WIN_C - 2)
    n_off = win_r
    off = jnp.arange(n_off)
    dr_idx = jnp.arange(win_r)[None, :] - off[:, None] + NA_WIN_R_MAX - 1
    dr_ok = (dr_idx >= 0) & (dr_idx <= 2 * NA_WIN_R_MAX - 2)
    bias = rpb[:, jnp.clip(dr_idx, 0, 2 * NA_WIN_R_MAX - 2)][:, :, :, dc_idx]
    keep = in_win[None, None, None] & dr_ok[None, :, :, None, None]
    bias = jnp.where(keep, bias, NEG_BIG)
    bias = bias.transpose(0, 1, 3, 2, 4).reshape(rpb.shape[0], n_off, GRID_W, win_r * GRID_W)
    return bias.astype(F32)


def _na(z, rpb, ctx_len, n_x):
    b, s, _ = z.shape
    rows = n_x // GRID_W
    win_r = min(NA_WIN_R_MAX, rows)
    bias = _na_bias_table(rpb, rows, win_r)
    nh = NA_HEADS
    rps = NA_ROWS_PER_STEP
    assert rows % rps == 0 and ctx_len % (rps * GRID_W) == 0
    ctx_blocks = ctx_len // (rps * GRID_W)
    return pl.pallas_call(
        functools.partial(_na_kernel, scale=HEAD_DIM ** -0.5, ctx_len=ctx_len, rows=rows, win_r=win_r),
        out_shape=jax.ShapeDtypeStruct((b, n_x, nh * HEAD_DIM), BF16),
        grid=(b, nh, rows // rps),
        in_specs=[pl.BlockSpec((1, rps * GRID_W, HEAD_DIM), lambda bi, h, r: (bi, r + ctx_blocks, h)),
                  pl.BlockSpec((1, s, HEAD_DIM), lambda bi, h, r: (bi, 0, nh + h)),
                  pl.BlockSpec((1, s, HEAD_DIM), lambda bi, h, r: (bi, 0, 2 * nh + h)),
                  pl.BlockSpec((1, win_r, GRID_W, win_r * GRID_W), lambda bi, h, r: (h, 0, 0, 0))],
        out_specs=pl.BlockSpec((1, rps * GRID_W, HEAD_DIM), lambda bi, h, r: (bi, r, h)),
        scratch_shapes=[pltpu.VMEM((s, HEAD_DIM), BF16), pltpu.VMEM((s, HEAD_DIM), BF16)],
        compiler_params=_cparams(("parallel", "parallel", "arbitrary")),
        name="na",
    )(z, z, z, bias)


def _mla_prep_kernel(qup_ref, kvup_ref, kr_ref, cos_ref, sin_ref, qf_ref, kf_ref, vf_ref):
    cos = cos_ref[...]
    sin = sin_ref[...]
    kr = _rotate(kr_ref[0], cos, sin, MLA_ROPE // 4).astype(BF16)
    for h in range(MLA_HEADS):
        q0 = h * MLA_QK_PAD
        kv0 = h * (MLA_NOPE + MLA_V)
        qf_ref[0, :, q0:q0 + LANES] = qup_ref[0, :, q0:q0 + LANES].astype(BF16)
        qf_ref[0, :, q0 + LANES:q0 + 2 * LANES] = _rotate(qup_ref[0, :, q0 + LANES:q0 + 2 * LANES], cos, sin,
                                                          MLA_ROPE // 4).astype(BF16)
        kf_ref[0, :, q0:q0 + LANES] = kvup_ref[0, :, kv0:kv0 + MLA_NOPE].astype(BF16)
        kf_ref[0, :, q0 + LANES:q0 + 2 * LANES] = kr
        vf_ref[0, :, h * MLA_V:(h + 1) * MLA_V] = kvup_ref[0, :, kv0 + MLA_NOPE:kv0 + MLA_NOPE + MLA_V].astype(BF16)


def _mla_prep(qup, kvup, z, kr_col, cos, sin, ts=256):
    b, s, _ = qup.shape
    nh = MLA_HEADS
    return pl.pallas_call(
        _mla_prep_kernel,
        out_shape=[jax.ShapeDtypeStruct((b, s, nh * MLA_QK_PAD), BF16),
                   jax.ShapeDtypeStruct((b, s, nh * MLA_QK_PAD), BF16),
                   jax.ShapeDtypeStruct((b, s, nh * MLA_V), BF16)],
        grid=(b, s // ts),
        in_specs=[pl.BlockSpec((1, ts, nh * MLA_QK_PAD), lambda bi, i: (bi, i, 0)),
                  pl.BlockSpec((1, ts, nh * (MLA_NOPE + MLA_V)), lambda bi, i: (bi, i, 0)),
                  pl.BlockSpec((1, ts, LANES), lambda bi, i: (bi, i, kr_col)),
                  pl.BlockSpec((ts, LANES), lambda bi, i: (i, 0)),
                  pl.BlockSpec((ts, LANES), lambda bi, i: (i, 0))],
        out_specs=[pl.BlockSpec((1, ts, nh * MLA_QK_PAD), lambda bi, i: (bi, i, 0)),
                   pl.BlockSpec((1, ts, nh * MLA_QK_PAD), lambda bi, i: (bi, i, 0)),
                   pl.BlockSpec((1, ts, nh * MLA_V), lambda bi, i: (bi, i, 0))],
        compiler_params=_cparams(("parallel", "parallel")),
        name="mla_prep",
    )(qup, kvup, z, cos, sin)


def _topk_rows(x, k, order=None, payload=None):
    n, width = x.shape
    if order is None:
        order = lax.broadcasted_iota(I32, (n, width), 0).astype(F32)
    out_row = lax.broadcasted_iota(I32, (k, width), 0)
    vals = jnp.zeros((k, width), F32)
    poss = jnp.zeros((k, width), F32)
    pays = jnp.zeros((k, width), F32)
    for r in range(k):
        m = jnp.max(x, axis=0, keepdims=True)
        pos = jnp.min(jnp.where(x == m, order, 1e9), axis=0, keepdims=True)
        hit = order == pos
        vals = jnp.where(out_row == r, m, vals)
        poss = jnp.where(out_row == r, pos, poss)
        if payload is not None:
            pay = jnp.max(jnp.where(hit, payload, -1.0), axis=0, keepdims=True)
            pays = jnp.where(out_row == r, pay, pays)
        x = jnp.where(hit, -jnp.inf, x)
    return vals, poss, pays


def _candidate_groups(k):
    groups = [(0, k, 0, 1, k)]
    for b in range(1, SUBLANES):
        groups.append((0, SUBLANES, b, b + 1, min(SUBLANES, k // (b + 1))))
    groups.append((0, 1, SUBLANES, k, 1))
    return groups


def _peer_route_kernel(q_ref, keys_ref, gate_ref, eidx_ref):
    kk = PEER_TOPK
    nt = (((1,), (1,)), ((), ()))
    width = q_ref.shape[0]
    sub8 = lax.broadcasted_iota(I32, (SUBLANES, width), 0).astype(F32)
    sub16 = lax.broadcasted_iota(I32, (kk, width), 0).astype(F32)
    for h in range(PEER_HEADS):
        sv, si = [], []
        for p in range(2):
            j = 2 * h + p
            qj = q_ref[:, j * PEER_HALF:(j + 1) * PEER_HALF].astype(BF16)
            s = lax.dot_general(keys_ref[j], qj, nt, preferred_element_type=F32)
            v, pos, _ = _topk_rows(s, kk)
            sv.append(v)
            si.append(pos)
        cand, flat, cidx = [], [], []
        for a_lo, a_hi, b_lo, b_hi, n_valid in _candidate_groups(kk):
            c = sv[0][a_lo:a_hi, :] + sv[1][b_lo:b_hi, :]
            ci = si[0][a_lo:a_hi, :] * float(PEER_N_KEYS) + si[1][b_lo:b_hi, :]
            if b_hi - b_lo == 1:
                rows_a = sub16 if a_hi - a_lo == kk else sub8
                fl = rows_a * float(kk) + float(b_lo)
                if n_valid < a_hi - a_lo:
                    c = jnp.where(rows_a < float(n_valid), c, -jnp.inf)
            else:
                fl = sub8 + float(a_lo * kk + b_lo)
            cand.append(c)
            flat.append(fl)
            cidx.append(ci)
        top_s, _, eidx = _topk_rows(jnp.concatenate(cand, axis=0), kk, order=jnp.concatenate(flat, axis=0),
                                    payload=jnp.concatenate(cidx, axis=0))
        e = jnp.exp(top_s - top_s[0:1, :])
        gate_ref[h] = e / jnp.sum(e, axis=0, keepdims=True)
        eidx_ref[h] = eidx.astype(I32)


def _peer_route(q, keys, tm=256):
    t = q.shape[0]
    nh, kk = PEER_HEADS, PEER_TOPK
    gate, eidx = pl.pallas_call(
        _peer_route_kernel,
        out_shape=[jax.ShapeDtypeStruct((nh, kk, t), F32), jax.ShapeDtypeStruct((nh, kk, t), I32)],
        grid=(t // tm,),
        in_specs=[pl.BlockSpec((tm, q.shape[1]), lambda i: (i, 0)),
                  pl.BlockSpec(keys.shape, lambda i: (0, 0, 0))],
        out_specs=[pl.BlockSpec((nh, kk, tm), lambda i: (0, 0, i)),
                   pl.BlockSpec((nh, kk, tm), lambda i: (0, 0, i))],
        compiler_params=_cparams(("parallel",)),
        name="peer_route",
    )(q, keys)
    return gate.reshape(nh * kk, t).T, eidx.reshape(nh * kk, t).T


def _gelu_tanh(x):
    return 0.5 * x * (1.0 + jnp.tanh(math.sqrt(2.0 / math.pi) * (x + 0.044715 * x * x * x)))


def _pack_tables(u, v):
    ub = lax.bitcast_convert_type(u.astype(BF16), jnp.uint16).astype(jnp.uint32)
    vb = lax.bitcast_convert_type(v.astype(BF16), jnp.uint16).astype(jnp.uint32)
    return (ub | (vb << 16))[:, None, :]


def _peer_expert_kernel(idx_cur, idx_nxt, f_ref, gate_ref, h_ref, mod_ref, fw_ref, uv_hbm, o_ref,
                        buf, sem, *, tb, n_sel, final_norm):
    i = pl.program_id(0)
    n = pl.num_programs(0)
    slot = i % 2
    nxt = 1 - slot
    groups = n_sel // SUBLANES
    half = groups // 2

    def issue(idx_ref, s, t, g_lo, g_hi):
        for g in range(g_lo, g_hi):
            for k in range(SUBLANES):
                row = idx_ref[t * n_sel + (g * SUBLANES + k)]
                pltpu.make_async_copy(uv_hbm.at[row], buf.at[s, t * groups + g, pl.ds(k, 1), :],
                                      sem.at[s]).start(priority=k % 2)

    def wait(s):
        pltpu.make_async_copy(buf.at[s], buf.at[s], sem.at[s]).wait()

    @pl.when(i == 0)
    def _():
        def first(t, carry):
            issue(idx_cur, 0, t, 0, groups)
            return carry
        lax.fori_loop(0, tb, first, 0)

    wait(slot)

    x = f_ref[...]
    tok_row = lax.broadcasted_iota(I32, (tb, n_sel), 0)
    nt = (((1,), (1,)), ((), ()))
    d = buf.shape[-1]

    def words(t):
        return buf[slot, pl.ds(pl.multiple_of(t * groups, groups), groups)].reshape(n_sel, d)

    def score(t, hh):
        u_t = lax.bitcast_convert_type(words(t) << 16, F32).astype(BF16)
        s = lax.dot_general(x, u_t, nt, preferred_element_type=F32)
        issue(idx_nxt, nxt, t, 0, half)
        return jnp.where(tok_row == t, s, hh)

    hh = lax.fori_loop(0, tb, score, jnp.zeros((tb, n_sel), F32), unroll=2)
    act = gate_ref[...] * _gelu_tanh(hh)

    def mix(t, acc):
        v_t = lax.bitcast_convert_type(words(t) & jnp.uint32(0xFFFF0000), F32).astype(BF16)
        wm = jnp.where(tok_row == t, act, 0.0).astype(BF16)
        y = jnp.dot(wm, v_t, preferred_element_type=F32)
        issue(idx_nxt, nxt, t, half, groups)
        return acc + y

    y = lax.fori_loop(0, tb, mix, jnp.zeros(o_ref.shape, F32), unroll=2)
    out = h_ref[...] + mod_ref[0, pl.ds(5, 1), :] * y
    if final_norm:
        out = out * lax.rsqrt(jnp.mean(out * out, axis=-1, keepdims=True) + EPS) * fw_ref[...]
    o_ref[...] = out

    @pl.when(i == n - 1)
    def _():
        wait(nxt)


def _peer_experts(f, gate, eidx, h, mod, uv, final_w, *, rows_per_batch, n_ctx_rows, final_norm, tb=16):
    t, d = f.shape
    n_sel = gate.shape[1]
    nb = mod.shape[0] - 1
    bpb = rows_per_batch // tb
    ctx_blocks = n_ctx_rows // tb
    n_blocks = t // tb
    eflat = eidx.reshape(t * n_sel)

    def mod_map(i):
        return (jnp.where(i % bpb < ctx_blocks, nb, i // bpb), 0, 0)

    smem = pltpu.SMEM
    return pl.pallas_call(
        functools.partial(_peer_expert_kernel, tb=tb, n_sel=n_sel, final_norm=final_norm),
        out_shape=jax.ShapeDtypeStruct((t, d), F32),
        grid=(n_blocks,),
        in_specs=[pl.BlockSpec((tb * n_sel,), lambda i: (i,), memory_space=smem),
                  pl.BlockSpec((tb * n_sel,), lambda i: (jnp.minimum(i + 1, n_blocks - 1),), memory_space=smem),
                  pl.BlockSpec((tb, d), lambda i: (i, 0)),
                  pl.BlockSpec((tb, n_sel), lambda i: (i, 0)),
                  pl.BlockSpec((tb, d), lambda i: (i, 0)),
                  pl.BlockSpec((1, 6, d), mod_map),
                  pl.BlockSpec((1, d), lambda i: (0, 0)),
                  pl.BlockSpec(memory_space=pl.ANY)],
        out_specs=pl.BlockSpec((tb, d), lambda i: (i, 0)),
        scratch_shapes=[pltpu.VMEM((2, tb * n_sel // SUBLANES, SUBLANES, d), jnp.uint32),
                        pltpu.SemaphoreType.DMA((2,))],
        compiler_params=_cparams(("arbitrary",)),
        name="peer_experts",
    )(eflat, eflat, f, gate, h, mod, final_w.reshape(1, d).astype(F32), uv)


def _peer(h, mod, norm_w, wq, keys, u, v, final_w, *, n_ctx_blocks, n_ctx_rows, final_norm, tm=256):
    b, r, d = h.shape
    q, f = _norm_matmul(h, norm_w, wq.astype(BF16), mod=mod, shift_row=3, n_ctx_blocks=n_ctx_blocks,
                        tm=tm, tn=wq.shape[1] // 2, emit_a=True, name="peer_query")
    keys2 = keys.reshape(PEER_HEADS * 2, PEER_N_KEYS, PEER_HALF).astype(BF16)
    gate, eidx = _peer_route(q.reshape(b * r, -1), keys2, tm=tm)
    out = _peer_experts(f.reshape(b * r, d), gate, eidx, h.reshape(b * r, d), mod, _pack_tables(u, v), final_w,
                        rows_per_batch=r, n_ctx_rows=n_ctx_rows, final_norm=final_norm)
    return out.reshape(b, r, d)


def kernel(x, c, ctx, c_ctx, w_mod, b_mod, norm1_w, norm2_w, ev_w_in, ev_w_out, gqa_q_norm, gqa_k_norm, hgrn_lb, hgrn_o_norm, od_w_in, od_w_out, na_rpb, mla_q_norm, mla_w_q_up, mla_kv_norm, mla_w_kv_up, peer_wq, peer_keys, peer_u, peer_v, final_norm_w):
    b, n_x, d = x.shape
    n_ctx = ctx.shape[1]
    depth = w_mod.shape[0]
    tm = 256
    assert n_ctx % tm == 0 and n_x % tm == 0 and depth == 2
    ctx_blocks = n_ctx // tm

    h = jnp.concatenate([ctx, x], axis=1)
    c_all = jnp.concatenate([c, c_ctx[None, :]], axis=0)
    m_pad = -(-c_all.shape[0] // 8) * 8
    c_pad = jnp.pad(c_all, ((0, m_pad - c_all.shape[0]), (0, 0)))
    lb_table = jnp.cumsum(jax.nn.softmax(hgrn_lb.astype(F32), axis=1), axis=1)

    mod0 = _modulation(c_pad, w_mod[0], b_mod[0])[:b + 1].reshape(b + 1, 6, d)
    z = _norm_matmul(h, norm1_w[0], ev_w_in[0].astype(BF16), mod=mod0, shift_row=0, n_ctx_blocks=ctx_blocks,
                     tm=tm, tn=ev_w_in.shape[2] // 4, name="in_proj0")
    cos_g, sin_g = _rope_tables(n_ctx, n_x, HEAD_DIM, HEAD_DIM // 4)
    qkv = _gqa_prep(z, gqa_q_norm[0], gqa_k_norm[0], cos_g, sin_g)
    ya = _attention(qkv, qkv, qkv, n_heads=GQA_HEADS, q_col=lambda hh: hh,
                    k_col=lambda hh: GQA_HEADS + hh // GQA_GROUP,
                    v_col=lambda hh: GQA_HEADS + GQA_KV_HEADS + hh // GQA_GROUP,
                    dk=HEAD_DIM, dv=HEAD_DIM, scale=HEAD_DIM ** -0.5, q_row_block0=0, n_q_rows=n_ctx + n_x,
                    n_ctx_q_blocks=ctx_blocks, ctx_len=n_ctx, tq=tm, name="gqa_attention")
    yb = _hgrn(z, GQA_IN_W // HEAD_DIM, lb_table[0, 0], lb_table[1, 0], hgrn_o_norm[0], n_ctx)
    h = _out_proj(ya, yb, ev_w_out[0], h, mod0, n_ctx_blocks=ctx_blocks, tm=tm)
    h = _peer(h, mod0, norm2_w[0], peer_wq[0], peer_keys[0], peer_u[0], peer_v[0], final_norm_w,
              n_ctx_blocks=ctx_blocks, n_ctx_rows=n_ctx, final_norm=False, tm=tm)

    mod1 = _modulation(c_pad, w_mod[1], b_mod[1])[:b + 1].reshape(b + 1, 6, d)
    odd_w = od_w_in[0]
    kr_pad = LANES - MLA_ROPE
    odd_w = jnp.pad(odd_w, ((0, 0), (0, kr_pad))).astype(BF16)
    z1 = _norm_matmul(h, norm1_w[1], odd_w, mod=mod1, shift_row=0, n_ctx_blocks=ctx_blocks, tm=tm,
                      tn=odd_w.shape[1] // 3, name="in_proj1")
    ya1 = _na(z1, na_rpb[0], n_ctx, n_x)

    qw = mla_w_q_up[0].reshape(MLA_Q_RANK, MLA_HEADS, MLA_NOPE + MLA_ROPE)
    qw = jnp.pad(qw, ((0, 0), (0, 0), (0, MLA_QK_PAD - MLA_NOPE - MLA_ROPE)))
    qw = qw.reshape(MLA_Q_RANK, MLA_HEADS * MLA_QK_PAD).astype(BF16)
    cq_col = NA_IN_W // MLA_Q_RANK
    qup = _norm_matmul(z1, mla_q_norm[0], qw, col_block=cq_col, tm=tm, tn=qw.shape[1], name="mla_q_up")
    kvup = _norm_matmul(z1, mla_kv_norm[0], mla_w_kv_up[0].astype(BF16), col_block=cq_col + 1, tm=tm,
                        tn=mla_w_kv_up.shape[2], name="mla_kv_up")
    cos_m, sin_m = _rope_tables(n_ctx, n_x, MLA_ROPE, MLA_ROPE // 4)
    qf, kf, vf = _mla_prep(qup, kvup, z1, (NA_IN_W + MLA_Q_RANK + MLA_KV_RANK) // LANES, cos_m, sin_m)
    yb1 = _attention(qf, kf, vf, n_heads=MLA_HEADS, q_col=lambda hh: hh, k_col=lambda hh: hh,
                     v_col=lambda hh: hh, dk=MLA_QK_PAD, dv=MLA_V, scale=(MLA_NOPE + MLA_ROPE) ** -0.5,
                     q_row_block0=ctx_blocks, n_q_rows=n_x, n_ctx_q_blocks=0, ctx_len=n_ctx, tq=tm,
                     name="mla_attention")
    hx = _out_proj(ya1, yb1, od_w_out[0], h, mod1, n_ctx_blocks=0, h_row_block0=ctx_blocks, tm=tm)
    return _peer(hx, mod1, norm2_w[1], peer_wq[1], peer_keys[1], peer_u[1], peer_v[1], final_norm_w,
                 n_ctx_blocks=0, n_ctx_rows=0, final_norm=True, tm=tm)
```

```python
import functools
import math

import jax
import jax.numpy as jnp
from jax import lax
from jax.experimental import pallas as pl
from jax.experimental.pallas import tpu as pltpu

F32 = jnp.float32
BF16 = jnp.bfloat16
I32 = jnp.int32

EPS = 1e-6
ROPE_THETA = 10000.0
GRID_W = 64
HEAD_DIM = 128
LANES = 128
SUBLANES = 8

GQA_HEADS = 8
GQA_KV_HEADS = 2
GQA_GROUP = GQA_HEADS // GQA_KV_HEADS
GQA_IN_W = (GQA_HEADS + 2 * GQA_KV_HEADS) * HEAD_DIM
HGRN_HEADS = 8
HGRN_CHUNK = 32
HGRN_HEADS_PER_STEP = 2
NA_HEADS = 8
NA_WIN_R_MAX = 8
NA_WIN_C = 16
NA_IN_W = 3 * NA_HEADS * HEAD_DIM
NA_ROWS_PER_STEP = 4
MLA_HEADS = 8
MLA_Q_RANK = 512
MLA_KV_RANK = 512
MLA_NOPE = 128
MLA_ROPE = 64
MLA_V = 128
MLA_QK_PAD = 256
PEER_HEADS = 8
PEER_N_KEYS = 128
PEER_TOPK = 16
PEER_HALF = 128

VMEM_LIMIT = 56 * 1024 * 1024
NEG_BIG = -1e30


def _cparams(sem):
    return pltpu.CompilerParams(dimension_semantics=sem, vmem_limit_bytes=VMEM_LIMIT)


def _mod_kernel(c_ref, w_ref, b_ref, o_ref):
    c = c_ref[...]
    a = (c * jax.nn.sigmoid(c)).astype(BF16)
    o_ref[...] = jnp.dot(a, w_ref[...].astype(BF16), preferred_element_type=F32) + b_ref[...]


def _modulation(c_all, w, b, tn=1024):
    m, d = c_all.shape
    n = w.shape[1]
    return pl.pallas_call(
        _mod_kernel,
        out_shape=jax.ShapeDtypeStruct((m, n), F32),
        grid=(n // tn,),
        in_specs=[pl.BlockSpec((m, d), lambda j: (0, 0)),
                  pl.BlockSpec((d, tn), lambda j: (0, j)),
                  pl.BlockSpec((1, tn), lambda j: (0, j))],
        out_specs=pl.BlockSpec((m, tn), lambda j: (0, j)),
        compiler_params=_cparams(("parallel",)),
        name="modulation",
    )(c_all, w, b.reshape(1, n))


def _nmm_kernel(*refs, modulate, shift_row, emit_a):
    if modulate:
        h_ref, nw_ref, mod_ref, w_ref = refs[:4]
        rest = refs[4:]
    else:
        h_ref, nw_ref, w_ref = refs[:3]
        mod_ref = None
        rest = refs[3:]
    if emit_a:
        o_ref, a_ref, a_s = rest
    else:
        o_ref, a_s = rest
        a_ref = None

    @pl.when(pl.program_id(2) == 0)
    def _():
        x = h_ref[0].astype(F32)
        y = x * lax.rsqrt(jnp.mean(x * x, axis=-1, keepdims=True) + EPS) * nw_ref[...]
        if modulate:
            shift = mod_ref[0, pl.ds(shift_row, 1), :]
            scale = mod_ref[0, pl.ds(shift_row + 1, 1), :]
            y = y * (1.0 + scale) + shift
        a_s[...] = y.astype(BF16)
        if emit_a:
            a_ref[0] = y.astype(BF16)

    o_ref[0] = jnp.dot(a_s[...], w_ref[...], preferred_element_type=F32).astype(o_ref.dtype)


def _norm_matmul(h, norm_w, w, *, mod=None, shift_row=0, n_ctx_blocks=0, col_block=0, row_block0=0,
                 n_rows=None, tm=256, tn=512, emit_a=False, out_dtype=F32, name="norm_matmul"):
    b, s, _ = h.shape
    k, n = w.shape
    n_rows = s if n_rows is None else n_rows
    nb = b
    modulate = mod is not None
    grid = (b, n_rows // tm, n // tn)
    in_specs = [pl.BlockSpec((1, tm, k), lambda bi, i, j: (bi, i + row_block0, col_block)),
                pl.BlockSpec((1, k), lambda bi, i, j: (0, 0))]
    args = [h, norm_w.reshape(1, k).astype(F32)]
    if modulate:
        in_specs.append(pl.BlockSpec((1, 6, k), lambda bi, i, j: (jnp.where(i < n_ctx_blocks, nb, bi), 0, 0)))
        args.append(mod)
    in_specs.append(pl.BlockSpec((k, tn), lambda bi, i, j: (0, j)))
    args.append(w)
    out_shape = [jax.ShapeDtypeStruct((b, n_rows, n), out_dtype)]
    out_specs = [pl.BlockSpec((1, tm, tn), lambda bi, i, j: (bi, i, j))]
    if emit_a:
        out_shape.append(jax.ShapeDtypeStruct((b, n_rows, k), BF16))
        out_specs.append(pl.BlockSpec((1, tm, k), lambda bi, i, j: (bi, i, 0)))
    res = pl.pallas_call(
        functools.partial(_nmm_kernel, modulate=modulate, shift_row=shift_row, emit_a=emit_a),
        out_shape=out_shape, grid=grid, in_specs=in_specs, out_specs=out_specs,
        scratch_shapes=[pltpu.VMEM((tm, k), BF16)],
        compiler_params=_cparams(("parallel", "parallel", "arbitrary")),
        name=name,
    )(*args)
    return res if emit_a else res[0]


def _outproj_kernel(ya_ref, yb_ref, wa_ref, wb_ref, h_ref, mod_ref, o_ref, *, gate_row):
    acc = jnp.dot(ya_ref[0], wa_ref[...], preferred_element_type=F32)
    acc = acc + jnp.dot(yb_ref[0], wb_ref[...], preferred_element_type=F32)
    gate = mod_ref[0, pl.ds(gate_row, 1), :]
    o_ref[0] = h_ref[0] + gate * acc


def _out_proj(ya, yb, w_out, h, mod, *, n_ctx_blocks, h_row_block0=0, tm=256, tn=1024):
    b, n_rows, ka = ya.shape
    kb = yb.shape[2]
    d = w_out.shape[1]
    nb = b
    wa = w_out[:ka].astype(BF16)
    wb = w_out[ka:].astype(BF16)
    return pl.pallas_call(
        functools.partial(_outproj_kernel, gate_row=2),
        out_shape=jax.ShapeDtypeStruct((b, n_rows, d), F32),
        grid=(b, n_rows // tm, d // tn),
        in_specs=[pl.BlockSpec((1, tm, ka), lambda bi, i, j: (bi, i, 0)),
                  pl.BlockSpec((1, tm, kb), lambda bi, i, j: (bi, i, 0)),
                  pl.BlockSpec((ka, tn), lambda bi, i, j: (0, j)),
                  pl.BlockSpec((kb, tn), lambda bi, i, j: (0, j)),
                  pl.BlockSpec((1, tm, tn), lambda bi, i, j: (bi, i + h_row_block0, j)),
                  pl.BlockSpec((1, 6, tn), lambda bi, i, j: (jnp.where(i < n_ctx_blocks, nb, bi), 0, j))],
        out_specs=pl.BlockSpec((1, tm, tn), lambda bi, i, j: (bi, i, j)),
        compiler_params=_cparams(("parallel", "parallel", "parallel")),
        name="out_proj",
    )(ya, yb, wa, wb, h, mod)


def _rope_tables(n_ctx, n_x, width, half):
    t = jnp.arange(n_x)
    lane = jnp.arange(LANES)
    inv = ROPE_THETA ** (-(lane % half).astype(F32) / half)
    pos = jnp.where(lane[None, :] < width // 2, (t // GRID_W)[:, None], (t % GRID_W)[:, None]).astype(F32)
    ang = pos * inv[None, :]
    valid = (lane < width)[None, :]
    cos = jnp.where(valid, jnp.cos(ang), 1.0)
    first = (lane % (2 * half)) < half
    sin = jnp.where(valid, jnp.where(first[None, :], -jnp.sin(ang), jnp.sin(ang)), 0.0)
    cos = jnp.concatenate([jnp.ones((n_ctx, LANES), F32), cos], axis=0)
    sin = jnp.concatenate([jnp.zeros((n_ctx, LANES), F32), sin], axis=0)
    return cos, sin


def _rotate(x, cos, sin, half):
    lane = lax.broadcasted_iota(I32, x.shape, 1)
    first = (lane % (2 * half)) < half
    partner = jnp.where(first, pltpu.roll(x, LANES - half, 1), pltpu.roll(x, half, 1))
    return x * cos + partner * sin


def _gqa_prep_kernel(z_ref, qw_ref, kw_ref, cos_ref, sin_ref, o_ref):
    cos = cos_ref[...]
    sin = sin_ref[...]
    for c in range(GQA_IN_W // HEAD_DIM):
        cols = slice(c * HEAD_DIM, (c + 1) * HEAD_DIM)
        x = z_ref[0, :, cols]
        if c < GQA_HEADS + GQA_KV_HEADS:
            w = qw_ref[...] if c < GQA_HEADS else kw_ref[...]
            y = x * lax.rsqrt(jnp.mean(x * x, axis=-1, keepdims=True) + EPS) * w
            x = _rotate(y, cos, sin, HEAD_DIM // 4)
        o_ref[0, :, cols] = x.astype(BF16)


def _gqa_prep(z, q_norm_w, k_norm_w, cos, sin, ts=256):
    b, s, _ = z.shape
    return pl.pallas_call(
        _gqa_prep_kernel,
        out_shape=jax.ShapeDtypeStruct((b, s, GQA_IN_W), BF16),
        grid=(b, s // ts),
        in_specs=[pl.BlockSpec((1, ts, GQA_IN_W), lambda bi, i: (bi, i, 0)),
                  pl.BlockSpec((1, HEAD_DIM), lambda bi, i: (0, 0)),
                  pl.BlockSpec((1, HEAD_DIM), lambda bi, i: (0, 0)),
                  pl.BlockSpec((ts, LANES), lambda bi, i: (i, 0)),
                  pl.BlockSpec((ts, LANES), lambda bi, i: (i, 0))],
        out_specs=pl.BlockSpec((1, ts, GQA_IN_W), lambda bi, i: (bi, i, 0)),
        compiler_params=_cparams(("parallel", "parallel")),
        name="gqa_prep",
    )(z, q_norm_w.reshape(1, HEAD_DIM), k_norm_w.reshape(1, HEAD_DIM), cos, sin)


def _attn_kernel(q_ref, k_ref, v_ref, o_ref, *, scale, n_ctx_q_blocks, ctx_len, q_axis):
    q = q_ref[0]

    def attend(n_keys):
        k = k_ref[0, pl.ds(0, n_keys), :]
        v = v_ref[0, pl.ds(0, n_keys), :]
        s = lax.dot_general(q, k, (((1,), (1,)), ((), ())), preferred_element_type=F32) * scale
        m = jnp.max(s, axis=-1, keepdims=True)
        p = jnp.exp(s - m)
        l = jnp.sum(p, axis=-1, keepdims=True)
        o = jnp.dot(p.astype(BF16), v, preferred_element_type=F32)
        o_ref[0] = (o / l).astype(o_ref.dtype)

    n_all = k_ref.shape[1]
    if n_ctx_q_blocks == 0:
        attend(n_all)
    else:
        qi = pl.program_id(q_axis)

        @pl.when(qi < n_ctx_q_blocks)
        def _():
            attend(ctx_len)

        @pl.when(qi >= n_ctx_q_blocks)
        def _():
            attend(n_all)


def _attention(q_arr, k_arr, v_arr, *, n_heads, q_col, k_col, v_col, dk, dv, scale, q_row_block0,
               n_q_rows, n_ctx_q_blocks, ctx_len, tq=256, name="attention"):
    b, s, _ = k_arr.shape
    return pl.pallas_call(
        functools.partial(_attn_kernel, scale=scale, n_ctx_q_blocks=n_ctx_q_blocks, ctx_len=ctx_len, q_axis=2),
        out_shape=jax.ShapeDtypeStruct((b, n_q_rows, n_heads * dv), BF16),
        grid=(b, n_heads, n_q_rows // tq),
        in_specs=[pl.BlockSpec((1, tq, dk), lambda bi, h, i: (bi, i + q_row_block0, q_col(h))),
                  pl.BlockSpec((1, s, dk), lambda bi, h, i: (bi, 0, k_col(h))),
                  pl.BlockSpec((1, s, dv), lambda bi, h, i: (bi, 0, v_col(h)))],
        out_specs=pl.BlockSpec((1, tq, dv), lambda bi, h, i: (bi, i, h)),
        compiler_params=_cparams(("parallel", "parallel", "parallel")),
        name=name,
    )(q_arr, k_arr, v_arr)


def _split3(x):
    hi = x.astype(BF16)
    r1 = x - hi.astype(F32)
    mid = r1.astype(BF16)
    lo = (r1 - mid.astype(F32)).astype(BF16)
    return hi, mid, lo


def _exact_dot(a_bf16, x):
    hi, mid, lo = _split3(x)
    return (jnp.dot(a_bf16, hi, preferred_element_type=F32)
            + jnp.dot(a_bf16, mid, preferred_element_type=F32)
            + jnp.dot(a_bf16, lo, preferred_element_type=F32))


def _hgrn_kernel(q_ref, ff_ref, fb_ref, i_ref, g_ref, lbf_ref, lbb_ref, ow_ref, o_ref,
                 of_s, ob_s, cum_s, kk_s, vv_s, st_s, *, n_ctx_chunks, n_chunks):
    c_len = HGRN_CHUNK
    nblk = c_len // SUBLANES
    hps = HGRN_HEADS_PER_STEP
    row = lax.broadcasted_iota(I32, (c_len, c_len), 0)
    col = lax.broadcasted_iota(I32, (c_len, c_len), 1)
    tri_f = (col <= row).astype(BF16)
    tri_b = (col >= row).astype(BF16)
    sub = lax.broadcasted_iota(I32, (SUBLANES, HEAD_DIM), 0)

    def chunk(r0, f_ref, lb_ref, forward, hh):
        ci = (0 if forward else hps) + hh
        lanes = slice(hh * HEAD_DIM, (hh + 1) * HEAD_DIM)
        rows = pl.ds(r0, c_len)
        q = q_ref[0, rows, lanes]
        v = i_ref[0, rows, lanes]
        lb = lb_ref[hh]
        sg = jax.nn.sigmoid(f_ref[0, rows, lanes])
        k = (1.0 - lb) * (1.0 - sg)
        g = jnp.log(lb + (1.0 - lb) * sg)
        cum = _exact_dot(tri_f if forward else tri_b, g)
        total = cum[c_len - 1:c_len, :] if forward else cum[0:1, :]
        cum_s[ci] = cum
        kk_s[ci] = k
        vv_s[ci] = v
        qb = [q[SUBLANES * j:SUBLANES * (j + 1)] for j in range(nblk)]
        cb = [cum[SUBLANES * j:SUBLANES * (j + 1)] for j in range(nblk)]
        ob = [jnp.zeros((SUBLANES, HEAD_DIM), F32) for _ in range(nblk)]
        for s in range(c_len):
            js = s // SUBLANES
            cs = cum_s[ci, pl.ds(s, 1), :]
            ks = kk_s[ci, pl.ds(s, 1), :]
            vs = vv_s[ci, pl.ds(s, 1), :]
            for j in (range(js, nblk) if forward else range(0, js + 1)):
                dlt = cb[j] - cs
                if j == js:
                    keep = (sub >= s - SUBLANES * js) if forward else (sub <= s - SUBLANES * js)
                    dlt = jnp.where(keep, dlt, -jnp.inf)
                a = jnp.sum(qb[j] * ks * jnp.exp(dlt), axis=-1, keepdims=True)
                ob[j] = ob[j] + a * vs
        o = jnp.concatenate(ob, axis=0)
        st = st_s[ci]
        qd = (q * jnp.exp(cum)).astype(BF16)
        o = o + lax.dot_general(qd, st.astype(BF16), (((1,), (1,)), ((), ())), preferred_element_type=F32)
        kd = (k * jnp.exp(total - cum)).astype(BF16)
        st_s[ci] = st * jnp.exp(total) + lax.dot_general(v.astype(BF16), kd, (((0,), (0,)), ((), ())),
                                                        preferred_element_type=F32)
        return o

    st_s[...] = jnp.zeros_like(st_s)

    def body(j, carry):
        rf = pl.multiple_of(j * c_len, c_len)
        cbk = jnp.where(j < n_ctx_chunks, n_ctx_chunks - 1 - j, n_chunks - 1 - (j - n_ctx_chunks))
        rb = pl.multiple_of(cbk * c_len, c_len)
        for hh in range(hps):
            lanes = slice(hh * HEAD_DIM, (hh + 1) * HEAD_DIM)
            of_s[pl.ds(rf, c_len), lanes] = chunk(rf, ff_ref, lbf_ref, True, hh)
            ob_s[pl.ds(rb, c_len), lanes] = chunk(rb, fb_ref, lbb_ref, False, hh)
        return carry

    lax.fori_loop(0, n_chunks, body, 0)

    for hh in range(hps):
        lanes = slice(hh * HEAD_DIM, (hh + 1) * HEAD_DIM)
        o = of_s[:, lanes] + ob_s[:, lanes]
        y = o * lax.rsqrt(jnp.mean(o * o, axis=-1, keepdims=True) + EPS) * ow_ref[hh]
        gate = g_ref[0, :, lanes]
        o_ref[0, :, lanes] = (y * (gate * jax.nn.sigmoid(gate))).astype(o_ref.dtype)


def _hgrn(z, col0, lb_fw, lb_bw, o_norm_w, ctx_len):
    b, s, _ = z.shape
    nh = HGRN_HEADS
    hps = HGRN_HEADS_PER_STEP
    wide = hps * HEAD_DIM
    assert nh % hps == 0 and col0 % hps == 0
    blk = lambda grp: pl.BlockSpec((1, s, wide), lambda bi, hp: (bi, 0, (col0 + grp * nh) // hps + hp))
    vec = pl.BlockSpec((hps, 1, HEAD_DIM), lambda bi, hp: (hp, 0, 0))
    return pl.pallas_call(
        functools.partial(_hgrn_kernel, n_ctx_chunks=ctx_len // HGRN_CHUNK, n_chunks=s // HGRN_CHUNK),
        out_shape=jax.ShapeDtypeStruct((b, s, nh * HEAD_DIM), BF16),
        grid=(b, nh // hps),
        in_specs=[blk(0), blk(1), blk(2), blk(3), blk(4), vec, vec, vec],
        out_specs=pl.BlockSpec((1, s, wide), lambda bi, hp: (bi, 0, hp)),
        scratch_shapes=[pltpu.VMEM((s, wide), F32),
                        pltpu.VMEM((s, wide), F32),
                        pltpu.VMEM((2 * hps, HGRN_CHUNK, HEAD_DIM), F32),
                        pltpu.VMEM((2 * hps, HGRN_CHUNK, HEAD_DIM), F32),
                        pltpu.VMEM((2 * hps, HGRN_CHUNK, HEAD_DIM), F32),
                        pltpu.VMEM((2 * hps, HEAD_DIM, HEAD_DIM), F32)],
        compiler_params=_cparams(("parallel", "parallel")),
        name="hgrn",
    )(z, z, z, z, z, lb_fw.reshape(nh, 1, HEAD_DIM), lb_bw.reshape(nh, 1, HEAD_DIM),
      o_norm_w.reshape(nh, 1, HEAD_DIM))


def _na_kernel(q_ref, k_ref, v_ref, bias_ref, o_ref, kb_s, vb_s, *, scale, ctx_len, rows, win_r):
    rb = pl.program_id(2)

    @pl.when(rb == 0)
    def _():
        kb_s[...] = k_ref[0].astype(BF16)
        vb_s[...] = v_ref[0].astype(BF16)

    kc = kb_s[pl.ds(0, ctx_len), :]
    vc = vb_s[pl.ds(0, ctx_len), :]
    nt = (((1,), (1,)), ((), ()))
    for i in range(NA_ROWS_PER_STEP):
        r = rb * NA_ROWS_PER_STEP + i
        r0 = jnp.clip(r - win_r // 2, 0, rows - win_r)
        band = pl.ds(pl.multiple_of(ctx_len + r0 * GRID_W, GRID_W), win_r * GRID_W)
        q = q_ref[0, i * GRID_W:(i + 1) * GRID_W, :].astype(BF16)
        kb = kb_s[band, :]
        vb = vb_s[band, :]
        s_ctx = lax.dot_general(q, kc, nt, preferred_element_type=F32) * scale
        s_loc = lax.dot_general(q, kb, nt, preferred_element_type=F32) * scale + bias_ref[0, r - r0]
        m = jnp.maximum(jnp.max(s_ctx, axis=-1, keepdims=True), jnp.max(s_loc, axis=-1, keepdims=True))
        p_ctx = jnp.exp(s_ctx - m)
        p_loc = jnp.exp(s_loc - m)
        l = jnp.sum(p_ctx, axis=-1, keepdims=True) + jnp.sum(p_loc, axis=-1, keepdims=True)
        o = jnp.dot(p_ctx.astype(BF16), vc, preferred_element_type=F32)
        o = o + jnp.dot(p_loc.astype(BF16), vb, preferred_element_type=F32)
        o_ref[0, i * GRID_W:(i + 1) * GRID_W, :] = (o / l).astype(o_ref.dtype)


def _na_bias_table(rpb, rows, win_r):
    cols = jnp.arange(GRID_W)
    col_start = jnp.clip(cols - NA_WIN_C // 2, 0, GRID_W - NA_WIN_C)
    in_win = (cols[None, :] >= col_start[:, None]) & (cols[None, :] < col_start[:, None] + NA_WIN_C)
    dc_idx = jnp.clip(cols[None, :] - cols[:, None] + NA_WIN_C - 1, 0, 2 * NA_WIN_C - 2)
    n_off = win_r
    off = jnp.arange(n_off)
    dr_idx = jnp.arange(win_r)[None, :] - off[:, None] + NA_WIN_R_MAX - 1
    dr_ok = (dr_idx >= 0) & (dr_idx <= 2 * NA_WIN_R_MAX - 2)
    bias = rpb[:, jnp.clip(dr_idx, 0, 2 * NA_WIN_R_MAX - 2)][:, :, :, dc_idx]
    keep = in_win[None, None, None] & dr_ok[None, :, :, None, None]
    bias = jnp.where(keep, bias, NEG_BIG)
    bias = bias.transpose(0, 1, 3, 2, 4).reshape(rpb.shape[0], n_off, GRID_W, win_r * GRID_W)
    return bias.astype(F32)


def _na(z, rpb, ctx_len, n_x):
    b, s, _ = z.shape
    rows = n_x // GRID_W
    win_r = min(NA_WIN_R_MAX, rows)
    bias = _na_bias_table(rpb, rows, win_r)
    nh = NA_HEADS
    rps = NA_ROWS_PER_STEP
    assert rows % rps == 0 and ctx_len % (rps * GRID_W) == 0
    ctx_blocks = ctx_len // (rps * GRID_W)
    return pl.pallas_call(
        functools.partial(_na_kernel, scale=HEAD_DIM ** -0.5, ctx_len=ctx_len, rows=rows, win_r=win_r),
        out_shape=jax.ShapeDtypeStruct((b, n_x, nh * HEAD_DIM), BF16),
        grid=(b, nh, rows // rps),
        in_specs=[pl.BlockSpec((1, rps * GRID_W, HEAD_DIM), lambda bi, h, r: (bi, r + ctx_blocks, h)),
                  pl.BlockSpec((1, s, HEAD_DIM), lambda bi, h, r: (bi, 0, nh + h)),
                  pl.BlockSpec((1, s, HEAD_DIM), lambda bi, h, r: (bi, 0, 2 * nh + h)),
                  pl.BlockSpec((1, win_r, GRID_W, win_r * GRID_W), lambda bi, h, r: (h, 0, 0, 0))],
        out_specs=pl.BlockSpec((1, rps * GRID_W, HEAD_DIM), lambda bi, h, r: (bi, r, h)),
        scratch_shapes=[pltpu.VMEM((s, HEAD_DIM), BF16), pltpu.VMEM((s, HEAD_DIM), BF16)],
        compiler_params=_cparams(("parallel", "parallel", "arbitrary")),
        name="na",
    )(z, z, z, bias)


def _mla_prep_kernel(qup_ref, kvup_ref, kr_ref, cos_ref, sin_ref, qf_ref, kf_ref, vf_ref):
    cos = cos_ref[...]
    sin = sin_ref[...]
    kr = _rotate(kr_ref[0], cos, sin, MLA_ROPE // 4).astype(BF16)
    for h in range(MLA_HEADS):
        q0 = h * MLA_QK_PAD
        kv0 = h * (MLA_NOPE + MLA_V)
        qf_ref[0, :, q0:q0 + LANES] = qup_ref[0, :, q0:q0 + LANES].astype(BF16)
        qf_ref[0, :, q0 + LANES:q0 + 2 * LANES] = _rotate(qup_ref[0, :, q0 + LANES:q0 + 2 * LANES], cos, sin,
                                                          MLA_ROPE // 4).astype(BF16)
        kf_ref[0, :, q0:q0 + LANES] = kvup_ref[0, :, kv0:kv0 + MLA_NOPE].astype(BF16)
        kf_ref[0, :, q0 + LANES:q0 + 2 * LANES] = kr
        vf_ref[0, :, h * MLA_V:(h + 1) * MLA_V] = kvup_ref[0, :, kv0 + MLA_NOPE:kv0 + MLA_NOPE + MLA_V].astype(BF16)


def _mla_prep(qup, kvup, z, kr_col, cos, sin, ts=256):
    b, s, _ = qup.shape
    nh = MLA_HEADS
    return pl.pallas_call(
        _mla_prep_kernel,
        out_shape=[jax.ShapeDtypeStruct((b, s, nh * MLA_QK_PAD), BF16),
                   jax.ShapeDtypeStruct((b, s, nh * MLA_QK_PAD), BF16),
                   jax.ShapeDtypeStruct((b, s, nh * MLA_V), BF16)],
        grid=(b, s // ts),
        in_specs=[pl.BlockSpec((1, ts, nh * MLA_QK_PAD), lambda bi, i: (bi, i, 0)),
                  pl.BlockSpec((1, ts, nh * (MLA_NOPE + MLA_V)), lambda bi, i: (bi, i, 0)),
                  pl.BlockSpec((1, ts, LANES), lambda bi, i: (bi, i, kr_col)),
                  pl.BlockSpec((ts, LANES), lambda bi, i: (i, 0)),
                  pl.BlockSpec((ts, LANES), lambda bi, i: (i, 0))],
        out_specs=[pl.BlockSpec((1, ts, nh * MLA_QK_PAD), lambda bi, i: (bi, i, 0)),
                   pl.BlockSpec((1, ts, nh * MLA_QK_PAD), lambda bi, i: (bi, i, 0)),
                   pl.BlockSpec((1, ts, nh * MLA_V), lambda bi, i: (bi, i, 0))],
        compiler_params=_cparams(("parallel", "parallel")),
        name="mla_prep",
    )(qup, kvup, z, cos, sin)


def _topk_rows(x, k, order=None, payload=None):
    n, width = x.shape
    if order is None:
        order = lax.broadcasted_iota(I32, (n, width), 0).astype(F32)
    out_row = lax.broadcasted_iota(I32, (k, width), 0)
    vals = jnp.zeros((k, width), F32)
    poss = jnp.zeros((k, width), F32)
    pays = jnp.zeros((k, width), F32)
    for r in range(k):
        m = jnp.max(x, axis=0, keepdims=True)
        pos = jnp.min(jnp.where(x == m, order, 1e9), axis=0, keepdims=True)
        hit = order == pos
        vals = jnp.where(out_row == r, m, vals)
        poss = jnp.where(out_row == r, pos, poss)
        if payload is not None:
            pay = jnp.max(jnp.where(hit, payload, -1.0), axis=0, keepdims=True)
            pays = jnp.where(out_row == r, pay, pays)
        x = jnp.where(hit, -jnp.inf, x)
    return vals, poss, pays


def _candidate_groups(k):
    groups = [(0, k, 0, 1, k)]
    for b in range(1, SUBLANES):
        groups.append((0, SUBLANES, b, b + 1, min(SUBLANES, k // (b + 1))))
    groups.append((0, 1, SUBLANES, k, 1))
    return groups


def _peer_route_kernel(q_ref, keys_ref, gate_ref, eidx_ref):
    kk = PEER_TOPK
    nt = (((1,), (1,)), ((), ()))
    width = q_ref.shape[0]
    sub8 = lax.broadcasted_iota(I32, (SUBLANES, width), 0).astype(F32)
    sub16 = lax.broadcasted_iota(I32, (kk, width), 0).astype(F32)
    for h in range(PEER_HEADS):
        sv, si = [], []
        for p in range(2):
            j = 2 * h + p
            qj = q_ref[:, j * PEER_HALF:(j + 1) * PEER_HALF].astype(BF16)
            s = lax.dot_general(keys_ref[j], qj, nt, preferred_element_type=F32)
            v, pos, _ = _topk_rows(s, kk)
            sv.append(v)
            si.append(pos)
        cand, flat, cidx = [], [], []
        for a_lo, a_hi, b_lo, b_hi, n_valid in _candidate_groups(kk):
            c = sv[0][a_lo:a_hi, :] + sv[1][b_lo:b_hi, :]
            ci = si[0][a_lo:a_hi, :] * float(PEER_N_KEYS) + si[1][b_lo:b_hi, :]
            if b_hi - b_lo == 1:
                rows_a = sub16 if a_hi - a_lo == kk else sub8
                fl = rows_a * float(kk) + float(b_lo)
                if n_valid < a_hi - a_lo:
                    c = jnp.where(rows_a < float(n_valid), c, -jnp.inf)
            else:
                fl = sub8 + float(a_lo * kk + b_lo)
            cand.append(c)
            flat.append(fl)
            cidx.append(ci)
        top_s, _, eidx = _topk_rows(jnp.concatenate(cand, axis=0), kk, order=jnp.concatenate(flat, axis=0),
                                    payload=jnp.concatenate(cidx, axis=0))
        e = jnp.exp(top_s - top_s[0:1, :])
        gate_ref[h] = e / jnp.sum(e, axis=0, keepdims=True)
        eidx_ref[h] = eidx.astype(I32)


def _peer_route(q, keys, tm=256):
    t = q.shape[0]
    nh, kk = PEER_HEADS, PEER_TOPK
    gate, eidx = pl.pallas_call(
        _peer_route_kernel,
        out_shape=[jax.ShapeDtypeStruct((nh, kk, t), F32), jax.ShapeDtypeStruct((nh, kk, t), I32)],
        grid=(t // tm,),
        in_specs=[pl.BlockSpec((tm, q.shape[1]), lambda i: (i, 0)),
                  pl.BlockSpec(keys.shape, lambda i: (0, 0, 0))],
        out_specs=[pl.BlockSpec((nh, kk, tm), lambda i: (0, 0, i)),
                   pl.BlockSpec((nh, kk, tm), lambda i: (0, 0, i))],
        compiler_params=_cparams(("parallel",)),
        name="peer_route",
    )(q, keys)
    return gate.reshape(nh * kk, t).T, eidx.reshape(nh * kk, t).T


def _gelu_tanh(x):
    return 0.5 * x * (1.0 + jnp.tanh(math.sqrt(2.0 / math.pi) * (x + 0.044715 * x * x * x)))


def _pack_tables(u, v):
    ub = lax.bitcast_convert_type(u.astype(BF16), jnp.uint16).astype(jnp.uint32)
    vb = lax.bitcast_convert_type(v.astype(BF16), jnp.uint16).astype(jnp.uint32)
    return (ub | (vb << 16))[:, None, :]


def _peer_expert_kernel(idx_cur, idx_nxt, f_ref, gate_ref, h_ref, mod_ref, fw_ref, uv_hbm, o_ref,
                        buf, sem, *, tb, n_sel, final_norm):
    i = pl.program_id(0)
    n = pl.num_programs(0)
    slot = i % 2
    nxt = 1 - slot
    groups = n_sel // SUBLANES
    half = groups // 2

    def issue(idx_ref, s, t, g_lo, g_hi):
        for g in range(g_lo, g_hi):
            for k in range(SUBLANES):
                row = idx_ref[t * n_sel + (g * SUBLANES + k)]
                pltpu.make_async_copy(uv_hbm.at[row], buf.at[s, t * groups + g, pl.ds(k, 1), :],
                                      sem.at[s]).start(priority=k % 2)

    def wait(s):
        pltpu.make_async_copy(buf.at[s], buf.at[s], sem.at[s]).wait()

    @pl.when(i == 0)
    def _():
        def first(t, carry):
            issue(idx_cur, 0, t, 0, groups)
            return carry
        lax.fori_loop(0, tb, first, 0)

    wait(slot)

    x = f_ref[...]
    tok_row = lax.broadcasted_iota(I32, (tb, n_sel), 0)
    nt = (((1,), (1,)), ((), ()))
    d = buf.shape[-1]

    def words(t):
        return buf[slot, pl.ds(t * groups, groups)].reshape(n_sel, d)

    def score(t, hh):
        u_t = lax.bitcast_convert_type(words(t) << 16, F32).astype(BF16)
        s = lax.dot_general(x, u_t, nt, preferred_element_type=F32)
        issue(idx_nxt, nxt, t, 0, half)
        return jnp.where(tok_row == t, s, hh)

    hh = jnp.zeros((tb, n_sel), F32)
    for t in range(tb):
        hh = score(t, hh)
    act = gate_ref[...] * _gelu_tanh(hh)

    def mix(t, acc):
        v_t = lax.bitcast_convert_type(words(t) & jnp.uint32(0xFFFF0000), F32).astype(BF16)
        wm = jnp.where(tok_row == t, act, 0.0).astype(BF16)
        y = jnp.dot(wm, v_t, preferred_element_type=F32)
        issue(idx_nxt, nxt, t, half, groups)
        return acc + y

    y = jnp.zeros(o_ref.shape, F32)
    for t in range(tb):
        y = mix(t, y)
    out = h_ref[...] + mod_ref[0, pl.ds(5, 1), :] * y
    if final_norm:
        out = out * lax.rsqrt(jnp.mean(out * out, axis=-1, keepdims=True) + EPS) * fw_ref[...]
    o_ref[...] = out

    @pl.when(i == n - 1)
    def _():
        wait(nxt)


def _peer_experts(f, gate, eidx, h, mod, uv, final_w, *, rows_per_batch, n_ctx_rows, final_norm, tb=16):
    t, d = f.shape
    n_sel = gate.shape[1]
    nb = mod.shape[0] - 1
    bpb = rows_per_batch // tb
    ctx_blocks = n_ctx_rows // tb
    n_blocks = t // tb
    eflat = eidx.reshape(t * n_sel)

    def mod_map(i):
        return (jnp.where(i % bpb < ctx_blocks, nb, i // bpb), 0, 0)

    smem = pltpu.SMEM
    return pl.pallas_call(
        functools.partial(_peer_expert_kernel, tb=tb, n_sel=n_sel, final_norm=final_norm),
        out_shape=jax.ShapeDtypeStruct((t, d), F32),
        grid=(n_blocks,),
        in_specs=[pl.BlockSpec((tb * n_sel,), lambda i: (i,), memory_space=smem),
                  pl.BlockSpec((tb * n_sel,), lambda i: (jnp.minimum(i + 1, n_blocks - 1),), memory_space=smem),
                  pl.BlockSpec((tb, d), lambda i: (i, 0)),
                  pl.BlockSpec((tb, n_sel), lambda i: (i, 0)),
                  pl.BlockSpec((tb, d), lambda i: (i, 0)),
                  pl.BlockSpec((1, 6, d), mod_map),
                  pl.BlockSpec((1, d), lambda i: (0, 0)),
                  pl.BlockSpec(memory_space=pl.ANY)],
        out_specs=pl.BlockSpec((tb, d), lambda i: (i, 0)),
        scratch_shapes=[pltpu.VMEM((2, tb * n_sel // SUBLANES, SUBLANES, d), jnp.uint32),
                        pltpu.SemaphoreType.DMA((2,))],
        compiler_params=_cparams(("arbitrary",)),
        name="peer_experts",
    )(eflat, eflat, f, gate, h, mod, final_w.reshape(1, d).astype(F32), uv)


def _peer(h, mod, norm_w, wq, keys, u, v, final_w, *, n_ctx_blocks, n_ctx_rows, final_norm, tm=256):
    b, r, d = h.shape
    q, f = _norm_matmul(h, norm_w, wq.astype(BF16), mod=mod, shift_row=3, n_ctx_blocks=n_ctx_blocks,
                        tm=tm, tn=wq.shape[1] // 2, emit_a=True, name="peer_query")
    keys2 = keys.reshape(PEER_HEADS * 2, PEER_N_KEYS, PEER_HALF).astype(BF16)
    gate, eidx = _peer_route(q.reshape(b * r, -1), keys2, tm=tm)
    out = _peer_experts(f.reshape(b * r, d), gate, eidx, h.reshape(b * r, d), mod, _pack_tables(u, v), final_w,
                        rows_per_batch=r, n_ctx_rows=n_ctx_rows, final_norm=final_norm)
    return out.reshape(b, r, d)


def kernel(x, c, ctx, c_ctx, w_mod, b_mod, norm1_w, norm2_w, ev_w_in, ev_w_out, gqa_q_norm, gqa_k_norm, hgrn_lb, hgrn_o_norm, od_w_in, od_w_out, na_rpb, mla_q_norm, mla_w_q_up, mla_kv_norm, mla_w_kv_up, peer_wq, peer_keys, peer_u, peer_v, final_norm_w):
    b, n_x, d = x.shape
    n_ctx = ctx.shape[1]
    depth = w_mod.shape[0]
    tm = 256
    assert n_ctx % tm == 0 and n_x % tm == 0 and depth == 2
    ctx_blocks = n_ctx // tm

    h = jnp.concatenate([ctx, x], axis=1)
    c_all = jnp.concatenate([c, c_ctx[None, :]], axis=0)
    m_pad = -(-c_all.shape[0] // 8) * 8
    c_pad = jnp.pad(c_all, ((0, m_pad - c_all.shape[0]), (0, 0)))
    lb_table = jnp.cumsum(jax.nn.softmax(hgrn_lb.astype(F32), axis=1), axis=1)

    mod0 = _modulation(c_pad, w_mod[0], b_mod[0])[:b + 1].reshape(b + 1, 6, d)
    z = _norm_matmul(h, norm1_w[0], ev_w_in[0].astype(BF16), mod=mod0, shift_row=0, n_ctx_blocks=ctx_blocks,
                     tm=tm, tn=ev_w_in.shape[2] // 4, name="in_proj0")
    cos_g, sin_g = _rope_tables(n_ctx, n_x, HEAD_DIM, HEAD_DIM // 4)
    qkv = _gqa_prep(z, gqa_q_norm[0], gqa_k_norm[0], cos_g, sin_g)
    ya = _attention(qkv, qkv, qkv, n_heads=GQA_HEADS, q_col=lambda hh: hh,
                    k_col=lambda hh: GQA_HEADS + hh // GQA_GROUP,
                    v_col=lambda hh: GQA_HEADS + GQA_KV_HEADS + hh // GQA_GROUP,
                    dk=HEAD_DIM, dv=HEAD_DIM, scale=HEAD_DIM ** -0.5, q_row_block0=0, n_q_rows=n_ctx + n_x,
                    n_ctx_q_blocks=ctx_blocks, ctx_len=n_ctx, tq=tm, name="gqa_attention")
    yb = _hgrn(z, GQA_IN_W // HEAD_DIM, lb_table[0, 0], lb_table[1, 0], hgrn_o_norm[0], n_ctx)
    h = _out_proj(ya, yb, ev_w_out[0], h, mod0, n_ctx_blocks=ctx_blocks, tm=tm)
    h = _peer(h, mod0, norm2_w[0], peer_wq[0], peer_keys[0], peer_u[0], peer_v[0], final_norm_w,
              n_ctx_blocks=ctx_blocks, n_ctx_rows=n_ctx, final_norm=False, tm=tm)

    mod1 = _modulation(c_pad, w_mod[1], b_mod[1])[:b + 1].reshape(b + 1, 6, d)
    odd_w = od_w_in[0]
    kr_pad = LANES - MLA_ROPE
    odd_w = jnp.pad(odd_w, ((0, 0), (0, kr_pad))).astype(BF16)
    z1 = _norm_matmul(h, norm1_w[1], odd_w, mod=mod1, shift_row=0, n_ctx_blocks=ctx_blocks, tm=tm,
                      tn=odd_w.shape[1] // 3, name="in_proj1")
    ya1 = _na(z1, na_rpb[0], n_ctx, n_x)

    qw = mla_w_q_up[0].reshape(MLA_Q_RANK, MLA_HEADS, MLA_NOPE + MLA_ROPE)
    qw = jnp.pad(qw, ((0, 0), (0, 0), (0, MLA_QK_PAD - MLA_NOPE - MLA_ROPE)))
    qw = qw.reshape(MLA_Q_RANK, MLA_HEADS * MLA_QK_PAD).astype(BF16)
    cq_col = NA_IN_W // MLA_Q_RANK
    qup = _norm_matmul(z1, mla_q_norm[0], qw, col_block=cq_col, tm=tm, tn=qw.shape[1], name="mla_q_up")
    kvup = _norm_matmul(z1, mla_kv_norm[0], mla_w_kv_up[0].astype(BF16), col_block=cq_col + 1, tm=tm,
                        tn=mla_w_kv_up.shape[2], name="mla_kv_up")
    cos_m, sin_m = _rope_tables(n_ctx, n_x, MLA_ROPE, MLA_ROPE // 4)
    qf, kf, vf = _mla_prep(qup, kvup, z1, (NA_IN_W + MLA_Q_RANK + MLA_KV_RANK) // LANES, cos_m, sin_m)
    yb1 = _attention(qf, kf, vf, n_heads=MLA_HEADS, q_col=lambda hh: hh, k_col=lambda hh: hh,
                     v_col=lambda hh: hh, dk=MLA_QK_PAD, dv=MLA_V, scale=(MLA_NOPE + MLA_ROPE) ** -0.5,
                     q_row_block0=ctx_blocks, n_q_rows=n_x, n_ctx_q_blocks=0, ctx_len=n_ctx, tq=tm,
                     name="mla_attention")
    hx = _out_proj(ya1, yb1, od_w_out[0], h, mod1, n_ctx_blocks=0, h_row_block0=ctx_blocks, tm=tm)
    return _peer(hx, mod1, norm2_w[1], peer_wq[1], peer_keys[1], peer_u[1], peer_v[1], final_norm_w,
                 n_ctx_blocks=0, n_ctx_rows=0, final_norm=True, tm=tm)
```
